```python
import jax, jax.numpy as jnp
from jax import lax
import numpy as np

D_MODEL = 4096
BATCH = 1
SEQ = 8192
DEPTH = 4

GRID_W = 64
CTX_LEN = 256
HEAD_DIM = 128
N_MIX_HEADS = D_MODEL // HEAD_DIM
A_Q_HEADS = N_MIX_HEADS // 2
A_KV_HEADS = A_Q_HEADS // 4
GQA_GROUP = A_Q_HEADS // A_KV_HEADS
B_GROUPS = N_MIX_HEADS // 2
C_GROUPS = 3 * N_MIX_HEADS // 4
D_GROUPS = N_MIX_HEADS // 4
CHUNK = 128
Q_BLOCK = 128
D_FF = 3 * D_MODEL // 2
ROPE_THETA = 10000.0
EPS = 1e-6
N_EVEN = (DEPTH + 1) // 2
N_ODD = DEPTH // 2

A_Q_W = A_Q_HEADS * HEAD_DIM
A_KV_W = A_KV_HEADS * HEAD_DIM
B_W = B_GROUPS * HEAD_DIM
C_W = C_GROUPS * HEAD_DIM
D_W = D_GROUPS * HEAD_DIM
EVEN_IN = A_Q_W + 2 * A_KV_W + 2 * B_W
EVEN_MIX = A_Q_W + B_W
ODD_IN = 3 * C_W + D_W
ODD_MIX = C_W + D_W

kernel_name = 'hybrid_diffusion_backbone_gqa_gmlp_shortconv_fourier'


def rmsnorm(x, g):
    x32 = x.astype(jnp.float32)
    y = x32 * lax.rsqrt(jnp.mean(x32 * x32, axis=-1, keepdims=True) + EPS)
    return y.astype(x.dtype) * g


def modulate(x, g, shift, scale):
    return rmsnorm(x, g) * (1.0 + scale) + shift


def dwconv3(x, w):
    xp = jnp.pad(x, ((0, 0), (1, 1), (0, 0)))
    return xp[:, :-2] * w[0] + xp[:, 1:-1] * w[1] + xp[:, 2:] * w[2]


def axial_rope_tables(n, dtype):
    rows = n // GRID_W
    row = jnp.repeat(jnp.arange(rows, dtype=jnp.float32), GRID_W)
    col = jnp.tile(jnp.arange(GRID_W, dtype=jnp.float32), rows)
    half = HEAD_DIM // 2
    inv_freq = ROPE_THETA ** (-jnp.arange(0, half, 2, dtype=jnp.float32) / half)
    ang = jnp.concatenate([row[:, None] * inv_freq, col[:, None] * inv_freq], axis=-1)
    return jnp.cos(ang).astype(dtype), jnp.sin(ang).astype(dtype)


def apply_rope(x, cos, sin):
    xr = x.reshape(x.shape[:-1] + (HEAD_DIM // 2, 2))
    x0, x1 = xr[..., 0], xr[..., 1]
    c = cos[None, :, None, :]
    s = sin[None, :, None, :]
    return jnp.stack([x0 * c - x1 * s, x0 * s + x1 * c], axis=-1).reshape(x.shape)


def attend(q, k, v):
    s = jnp.einsum('bqkgd,bskd->bkgqs', q, k, preferred_element_type=jnp.float32) * (HEAD_DIM ** -0.5)
    p = jax.nn.softmax(s, axis=-1).astype(v.dtype)
    return jnp.einsum('bkgqs,bskd->bqkgd', p, v)


def latent_attention(q, k_all, v_all):
    b, n = q.shape[:2]
    nb = n // Q_BLOCK
    qb = q.reshape(b, nb, Q_BLOCK, A_KV_HEADS, GQA_GROUP, HEAD_DIM).transpose(1, 0, 2, 3, 4, 5)
    o = lax.map(lambda qblk: attend(qblk, k_all, v_all), qb)
    return o.transpose(1, 0, 2, 3, 4, 5).reshape(b, n, A_Q_W)


def chunk_gmlp(u, v, norm_g, w_s, b_s):
    b, n, _ = u.shape
    u = jax.nn.gelu(u)
    v = rmsnorm(jax.nn.gelu(v).reshape(b, n, B_GROUPS, HEAD_DIM), norm_g.reshape(B_GROUPS, HEAD_DIM))
    v = v.reshape(b, n // CHUNK, CHUNK, B_GROUPS, HEAD_DIM)
    v = jnp.einsum('gpq,bnqgc->bnpgc', w_s, v) + b_s.T[:, :, None]
    return u * v.reshape(b, n, B_W)


def fourier_mix(f):
    b, n, _ = f.shape
    f32 = f.astype(jnp.float32).reshape(b, n, D_GROUPS, HEAD_DIM)
    y = jnp.fft.fft2(f32, axes=(1, 3), norm='ortho').real
    return y.reshape(b, n, D_W).astype(f.dtype)


def even_mixer(h, hc, w_in, w_out, q_g, k_g, gn_g, w_s, b_s, cos, sin, with_ctx_out):
    b, n, _ = h.shape
    lc = hc.shape[1]
    s1, s2, s3, s4 = A_Q_W, A_Q_W + A_KV_W, A_Q_W + 2 * A_KV_W, A_Q_W + 2 * A_KV_W + B_W
    q, k, v, u, gv = jnp.split(h @ w_in, [s1, s2, s3, s4], axis=-1)
    q = apply_rope(rmsnorm(q.reshape(b, n, A_Q_HEADS, HEAD_DIM), q_g), cos, sin)
    k = apply_rope(rmsnorm(k.reshape(b, n, A_KV_HEADS, HEAD_DIM), k_g), cos, sin)
    v = v.reshape(b, n, A_KV_HEADS, HEAD_DIM)
    if with_ctx_out:
        qc, kc, vc, uc, gvc = jnp.split(hc @ w_in, [s1, s2, s3, s4], axis=-1)
    else:
        kc, vc = jnp.split(hc @ w_in[:, s1:s3], 2, axis=-1)
    kc = rmsnorm(kc.reshape(b, lc, A_KV_HEADS, HEAD_DIM), k_g)
    vc = vc.reshape(b, lc, A_KV_HEADS, HEAD_DIM)
    k_all = jnp.concatenate([kc, k], axis=1)
    v_all = jnp.concatenate([vc, v], axis=1)
    attn = latent_attention(q.reshape(b, n, A_KV_HEADS, GQA_GROUP, HEAD_DIM), k_all, v_all)
    y = jnp.concatenate([attn, chunk_gmlp(u, gv, gn_g, w_s, b_s)], axis=-1) @ w_out
    if not with_ctx_out:
        return y, None
    qc = rmsnorm(qc.reshape(b, lc, A_KV_HEADS, GQA_GROUP, HEAD_DIM), q_g)
    attn_c = attend(qc, kc, vc).reshape(b, lc, A_Q_W)
    yc = jnp.concatenate([attn_c, chunk_gmlp(uc, gvc, gn_g, w_s, b_s)], axis=-1) @ w_out
    return y, yc


def odd_mixer(h, w_in, w_out, conv_w):
    bg, cg, hx, f = jnp.split(h @ w_in, [C_W, 2 * C_W, 3 * C_W], axis=-1)
    y_c = bg * dwconv3(cg * hx, conv_w)
    return jnp.concatenate([y_c, fourier_mix(f)], axis=-1) @ w_out


def conv_ffn(h, w_up, conv_w, conv_b, w_down):
    a = dwconv3(h @ w_up, conv_w) + conv_b
    g, u = jnp.split(a, 2, axis=-1)
    return (jax.nn.silu(g) * u) @ w_down


def setup_inputs(seed: int = 0) -> dict:
    key = jax.random.key(seed)
    ks = jax.random.split(key, 24)
    D = D_MODEL

    def nrm(k, shape, s):
        return jax.random.normal(k, shape, jnp.float32) * s

    return {
        'x': nrm(ks[0], (BATCH, SEQ, D), 1.0),
        'c': nrm(ks[1], (BATCH, D), 1.0),
        'ctx': nrm(ks[2], (BATCH, CTX_LEN, D), 1.0),
        'c_ctx': nrm(ks[3], (D,), 1.0),
        'w_ada': nrm(ks[4], (DEPTH, D, 6 * D), 0.5 * D ** -0.5),
        'b_ada': nrm(ks[5], (DEPTH, 6 * D), 0.02),
        'norm1_g': 1.0 + nrm(ks[6], (DEPTH, D), 0.02),
        'norm2_g': 1.0 + nrm(ks[7], (DEPTH, D), 0.02),
        'w_ffn_up': nrm(ks[8], (DEPTH, D, 2 * D_FF), D ** -0.5),
        'ffn_conv_w': nrm(ks[9], (DEPTH, 3, 2 * D_FF), 3 ** -0.5),
        'ffn_conv_b': nrm(ks[10], (DEPTH, 2 * D_FF), 0.02),
        'w_ffn_down': nrm(ks[11], (DEPTH, D_FF, D), D_FF ** -0.5),
        'w_in_a': nrm(ks[12], (N_EVEN, D, EVEN_IN), D ** -0.5),
        'w_out_a': nrm(ks[13], (N_EVEN, EVEN_MIX, D), EVEN_MIX ** -0.5),
        'q_norm_g': 1.0 + nrm(ks[14], (N_EVEN, HEAD_DIM), 0.02),
        'k_norm_g': 1.0 + nrm(ks[15], (N_EVEN, HEAD_DIM), 0.02),
        'gmlp_norm_g': 1.0 + nrm(ks[16], (N_EVEN, B_W), 0.02),
        'gmlp_w_s': nrm(ks[17], (N_EVEN, B_GROUPS, CHUNK, CHUNK), CHUNK ** -0.5),
        'gmlp_b_s': nrm(ks[18], (N_EVEN, B_GROUPS, CHUNK), 0.02),
        'w_in_c': nrm(ks[19], (N_ODD, D, ODD_IN), D ** -0.5),
        'w_out_c': nrm(ks[20], (N_ODD, ODD_MIX, D), ODD_MIX ** -0.5),
        'conv_w_c': nrm(ks[21], (N_ODD, 3, C_W), 3 ** -0.5),
    }


def reference(x, c, ctx, c_ctx, w_ada, b_ada, norm1_g, norm2_g, w_ffn_up, ffn_conv_w, ffn_conv_b,
              w_ffn_down, w_in_a, w_out_a, q_norm_g, k_norm_g, gmlp_norm_g, gmlp_w_s, gmlp_b_s,
              w_in_c, w_out_c, conv_w_c):
    n = x.shape[1]
    cos, sin = axial_rope_tables(n, x.dtype)
    xc = ctx
    s_lat = jax.nn.silu(c)
    s_ctx = jax.nn.silu(c_ctx)
    for i in range(DEPTH):
        ctx_needed_later = any(j % 2 == 0 for j in range(i + 1, DEPTH))
        sh1, sc1, g1, sh2, sc2, g2 = jnp.split((s_lat @ w_ada[i] + b_ada[i])[:, None, :], 6, axis=-1)
        if i % 2 == 0 or ctx_needed_later:
            csh1, csc1, cg1, csh2, csc2, cg2 = jnp.split(s_ctx @ w_ada[i] + b_ada[i], 6, axis=-1)
        h = modulate(x, norm1_g[i], sh1, sc1)
        if i % 2 == 0:
            e = i // 2
            hc = modulate(xc, norm1_g[i], csh1, csc1)
            y, yc = even_mixer(h, hc, w_in_a[e], w_out_a[e], q_norm_g[e], k_norm_g[e], gmlp_norm_g[e],
                               gmlp_w_s[e], gmlp_b_s[e], cos, sin, ctx_needed_later)
            x = x + g1 * y
            if ctx_needed_later:
                xc = xc + cg1 * yc
        else:
            o = i // 2
            x = x + g1 * odd_mixer(h, w_in_c[o], w_out_c[o], conv_w_c[o])
            if ctx_needed_later:
                hc = modulate(xc, norm1_g[i], csh1, csc1)
                xc = xc + cg1 * odd_mixer(hc, w_in_c[o], w_out_c[o], conv_w_c[o])
        x = x + g2 * conv_ffn(modulate(x, norm2_g[i], sh2, sc2), w_ffn_up[i], ffn_conv_w[i],
                              ffn_conv_b[i], w_ffn_down[i])
        if ctx_needed_later:
            xc = xc + cg2 * conv_ffn(modulate(xc, norm2_g[i], csh2, csc2), w_ffn_up[i], ffn_conv_w[i],
                                     ffn_conv_b[i], w_ffn_down[i])
    return x
```

```python
import functools
import math

import jax
import jax.numpy as jnp
from jax import lax
from jax.experimental import pallas as pl
from jax.experimental.pallas import tpu as pltpu

BF16 = jnp.bfloat16
F32 = jnp.float32

GRID_W = 64
ROPE_THETA = 10000.0
EPS = 1e-6
LANES = 128
HALO = 16
VMEM_CAP = 60 * 1024 * 1024
NEG_BIG = -1e30


def _pick(n, prefs):
    for p in prefs:
        if n % p == 0:
            return p
    return n


def _params(vmem_bytes, n_axes):
    return pltpu.CompilerParams(
        dimension_semantics=("arbitrary",) * n_axes,
        vmem_limit_bytes=int(min(VMEM_CAP, vmem_bytes)),
    )


def _ada_kernel(sb_ref, w_ref, b_ref, o_ref, *, tn, n_vec):
    for t in range(tn // LANES):
        sl = slice(t * LANES, (t + 1) * LANES)
        w = w_ref[:, sl]
        for r in range(n_vec):
            o_ref[r:r + 1, sl] = jnp.sum(w * sb_ref[r], axis=0, keepdims=True) + b_ref[:, sl]


def _ada(s_vecs, w_ada, b_ada):
    depth, d, n6 = w_ada.shape
    n_vec = s_vecs.shape[0]
    tn = _pick(n6, (512, 256, 128))
    sb = jnp.broadcast_to(s_vecs[:, :, None], (n_vec, d, LANES))
    return pl.pallas_call(
        functools.partial(_ada_kernel, tn=tn, n_vec=n_vec),
        grid=(depth, n6 // tn),
        in_specs=[
            pl.BlockSpec((n_vec, d, LANES), lambda l, j: (0, 0, 0)),
            pl.BlockSpec((None, d, tn), lambda l, j: (l, 0, j)),
            pl.BlockSpec((None, 1, tn), lambda l, j: (l, 0, j)),
        ],
        out_specs=pl.BlockSpec((None, n_vec, tn), lambda l, j: (l, 0, j)),
        out_shape=jax.ShapeDtypeStruct((depth, n_vec, n6), F32),
        compiler_params=_params(2 * d * tn * 4 + 4 * n_vec * d * LANES * 4 + (8 << 20), 2),
        name="ada",
    )(sb, w_ada, b_ada.reshape(depth, 1, n6))


def _modulate_kernel(x_ref, g_ref, sc_ref, sh_ref, o_ref):
    x = x_ref[...]
    ms = jnp.mean(x * x, axis=-1, keepdims=True)
    y = x * lax.rsqrt(ms + EPS)
    o_ref[...] = ((y * g_ref[...]) * (1.0 + sc_ref[...]) + sh_ref[...]).astype(BF16)


def _modulate(x, g, scale, shift):
    rows, d = x.shape
    tm = _pick(rows, (256, 128, 64, 32, 16))
    vec = pl.BlockSpec((1, d), lambda i: (0, 0))
    return pl.pallas_call(
        _modulate_kernel,
        grid=(rows // tm,),
        in_specs=[pl.BlockSpec((tm, d), lambda i: (i, 0)), vec, vec, vec],
        out_specs=pl.BlockSpec((tm, d), lambda i: (i, 0)),
        out_shape=jax.ShapeDtypeStruct((rows, d), BF16),
        compiler_params=_params(6 * tm * d * 4 + (8 << 20), 1),
        name="modulate",
    )(x, g.reshape(1, d), scale.reshape(1, d), shift.reshape(1, d))


def _mm_kernel(*refs, n_seg, halo, cast_w, epilogue, n_extra, n_out, tm, k_dim):
    it = iter(refs)
    lhs_ref = next(it)
    above_ref = next(it) if halo else None
    below_ref = next(it) if halo else None
    w_refs = [next(it) for _ in range(n_seg)]
    extra_refs = [next(it) for _ in range(n_extra)]
    out_refs = [next(it) for _ in range(n_out)]
    wb_refs = [next(it) for _ in range(n_seg)] if cast_w else w_refs
    lhs_scr = next(it) if halo else None

    i = pl.program_id(1)
    if cast_w:
        ck = _pick(k_dim, (256, 128, 64, 32, 16))

        @pl.when(i == 0)
        def _():
            def cast_chunk(c, carry):
                rows = pl.ds(pl.multiple_of(c * ck, ck), ck)
                for w_ref, wb_ref in zip(w_refs, wb_refs):
                    wb_ref[rows, :] = w_ref[rows, :].astype(BF16)
                return carry
            lax.fori_loop(0, k_dim // ck, cast_chunk, 0)

    if halo:
        n_i = pl.num_programs(1)
        zero = jnp.zeros(above_ref.shape, BF16)
        lhs_scr[0:HALO, :] = jnp.where(i > 0, above_ref[...], zero)
        lhs_scr[HALO:HALO + tm, :] = lhs_ref[...]
        lhs_scr[HALO + tm:HALO + tm + HALO, :] = jnp.where(i < n_i - 1, below_ref[...], zero)
        lhs = lhs_scr[...]
    else:
        lhs = lhs_ref[...]
    accs = [jnp.dot(lhs, wb_ref[...], preferred_element_type=F32) for wb_ref in wb_refs]
    epilogue(accs, extra_refs, out_refs)


def _mm(lhs, w, *, layer, seg_offs, n_cols, tn, tm, epilogue, extras, outs, halo=False, name="mm"):
    rows, k_dim = lhs.shape
    assert rows % tm == 0 and n_cols % tn == 0
    cast_w = w.dtype != BF16
    in_specs = [pl.BlockSpec((tm, k_dim), lambda j, i: (i, 0))]
    args = [lhs]
    if halo:
        assert tm % HALO == 0
        hb, nblk = tm // HALO, rows // HALO
        in_specs += [
            pl.BlockSpec((HALO, k_dim), lambda j, i: (jnp.maximum(i * hb - 1, 0), 0)),
            pl.BlockSpec((HALO, k_dim), lambda j, i: (jnp.minimum((i + 1) * hb, nblk - 1), 0)),
        ]
        args += [lhs, lhs]
    for off in seg_offs:
        assert off % tn == 0
        ob = off // tn
        if w.ndim == 3:
            in_specs.append(pl.BlockSpec((None, k_dim, tn), lambda j, i, ob=ob: (layer, 0, ob + j)))
        else:
            in_specs.append(pl.BlockSpec((k_dim, tn), lambda j, i, ob=ob: (0, ob + j)))
        args.append(w)
    for arr, bs, im in extras:
        in_specs.append(pl.BlockSpec(bs, im))
        args.append(arr)
    out_specs = [pl.BlockSpec(bs, im) for _, bs, im in outs]
    out_shape = [sds for sds, _, _ in outs]
    n_seg = len(seg_offs)
    scratch = []
    if cast_w:
        scratch += [pltpu.VMEM((k_dim, tn), BF16) for _ in range(n_seg)]
    rows_l = tm + 2 * HALO if halo else tm
    if halo:
        scratch.append(pltpu.VMEM((rows_l, k_dim), BF16))
    wbytes = w.dtype.itemsize
    est = (n_seg * k_dim * tn * (2 * wbytes + (2 if cast_w else 0))
           + 2 * tm * k_dim * 2 + (rows_l * k_dim * 2 if halo else 0)
           + (3 + 2 * n_seg) * rows_l * tn * 4 * 2 + (6 << 20))
    kern = functools.partial(
        _mm_kernel, n_seg=n_seg, halo=halo, cast_w=cast_w, epilogue=epilogue,
        n_extra=len(extras), n_out=len(outs), tm=tm, k_dim=k_dim)
    res = pl.pallas_call(
        kern,
        grid=(n_cols // tn, rows // tm),
        in_specs=in_specs,
        out_specs=out_specs,
        out_shape=out_shape,
        scratch_shapes=scratch,
        compiler_params=_params(est, 2),
        name=name,
    )(*args)
    return res


def _rms_lanes(x):
    return x * lax.rsqrt(jnp.mean(x * x, axis=-1, keepdims=True) + EPS)


def _epi_qk(accs, ex, outs, *, tn, scale):
    (acc,) = accs
    g_ref, cos_ref, sin_e_ref, sin_o_ref = ex
    (o_ref,) = outs
    g = g_ref[...]
    cos, sin_e, sin_o = cos_ref[...], sin_e_ref[...], sin_o_ref[...]
    for h in range(tn // LANES):
        hs = slice(h * LANES, (h + 1) * LANES)
        y = _rms_lanes(acc[:, hs]) * g
        r = y * cos + pltpu.roll(y, LANES - 1, 1) * sin_e + pltpu.roll(y, 1, 1) * sin_o
        o_ref[:, hs] = (r * scale if scale != 1.0 else r).astype(BF16)


def _epi_cast(accs, ex, outs):
    (acc,) = accs
    (o_ref,) = outs
    o_ref[...] = acc.astype(BF16)


def _epi_gmlp(accs, ex, outs, *, tm, tn):
    u, gv = accs
    gn_ref, ws_ref, bs_ref = ex
    (o_ref,) = outs
    gu = jax.nn.gelu(u)
    ggv = jax.nn.gelu(gv)
    nc = tm // LANES
    for g in range(tn // LANES):
        gs = slice(g * LANES, (g + 1) * LANES)
        vn = (_rms_lanes(ggv[:, gs]) * gn_ref[:, gs]).astype(BF16)
        ws = ws_ref[g].astype(BF16)
        bias = bs_ref[g]
        for c in range(nc):
            cs = slice(c * LANES, (c + 1) * LANES)
            res = jnp.dot(ws, vn[cs, :], preferred_element_type=F32)
            o_ref[cs, gs] = (gu[cs, gs] * (res + bias)).astype(BF16)


def _conv3_rows(acc, cw_ref, tm):
    n_r = acc.shape[0]
    prev = pltpu.roll(acc, 1, 0)[HALO:HALO + tm, :]
    nxt = pltpu.roll(acc, n_r - 1, 0)[HALO:HALO + tm, :]
    mid = acc[HALO:HALO + tm, :]
    return prev * cw_ref[0:1, :] + mid * cw_ref[1:2, :] + nxt * cw_ref[2:3, :]


def _epi_ffn_up(accs, ex, outs, *, tm):
    g, u = accs
    cwg_ref, cwu_ref, cbg_ref, cbu_ref = ex
    (o_ref,) = outs
    ag = _conv3_rows(g, cwg_ref, tm) + cbg_ref[...]
    au = _conv3_rows(u, cwu_ref, tm) + cbu_ref[...]
    o_ref[...] = (jax.nn.silu(ag) * au).astype(BF16)


def _epi_odd_conv(accs, ex, outs, *, tm):
    bg, cg, hx = accs
    (cw_ref,) = ex
    (o_ref,) = outs
    o_ref[...] = (bg[HALO:HALO + tm, :] * _conv3_rows(cg * hx, cw_ref, tm)).astype(BF16)


def _epi_residual(accs, ex, outs):
    (acc,) = accs
    x_ref, gate_ref = ex
    (o_ref,) = outs
    o_ref[...] = x_ref[...] + gate_ref[...] * acc


def _epi_chan_dft(accs, ex, outs, *, tn):
    (acc,) = accs
    (cs_ref,) = ex
    (o_ref,) = outs
    cs = cs_ref[...]
    for g in range(tn // LANES):
        gs = slice(g * LANES, (g + 1) * LANES)
        ab = jnp.dot(acc[:, gs].astype(BF16), cs, preferred_element_type=F32)
        o_ref[0, :, gs] = ab[:, :LANES].astype(BF16)
        o_ref[1, :, gs] = ab[:, LANES:].astype(BF16)


def _epi_scale(accs, ex, outs, *, scale):
    (acc,) = accs
    (o_ref,) = outs
    o_ref[...] = (acc * scale).astype(BF16)


_NT = (((1,), (1,)), ((), ()))
_TN = (((0,), (0,)), ((), ()))


def _attn_kernel(*refs, tq, tk, n_lat, group):
    if n_lat:
        q_ref, kc_ref, vc_ref, k_ref, v_ref, o_ref = refs
    else:
        q_ref, kc_ref, vc_ref, o_ref = refs
    r = group * tq
    q = jnp.concatenate([q_ref[:, g * LANES:(g + 1) * LANES] for g in range(group)], axis=0)

    def step(k_t, v_t, carry):
        m, l, acc = carry
        s = lax.dot_general(k_t, q, _NT, preferred_element_type=F32)
        m_new = jnp.maximum(m, jnp.max(s, axis=0, keepdims=True))
        alpha = jnp.exp2(m - m_new)
        p = jnp.exp2(s - m_new)
        l = alpha * l + jnp.sum(p, axis=0, keepdims=True)
        pv = lax.dot_general(v_t, p.astype(BF16), _TN, preferred_element_type=F32)
        return m_new, l, alpha * acc + pv

    carry = (jnp.full((1, r), NEG_BIG, F32), jnp.zeros((1, r), F32), jnp.zeros((LANES, r), F32))
    carry = step(kc_ref[...], vc_ref[...], carry)
    if n_lat:
        def body(t, c):
            rows = pl.ds(pl.multiple_of(t * tk, tk), tk)
            return step(k_ref[rows, :], v_ref[rows, :], c)
        carry = lax.fori_loop(0, n_lat, body, carry)
    _, l, acc = carry
    out_t = (acc * (1.0 / l)).T
    for g in range(group):
        o_ref[:, g * LANES:(g + 1) * LANES] = out_t[g * tq:(g + 1) * tq, :].astype(BF16)


def _attention(q, kc, vc, k_lat, v_lat, *, n_kv, group, tk):
    nq = q.shape[0]
    lc = kc.shape[0]
    tq = _pick(nq, (256, 128))
    gw = group * LANES
    ctx_spec = pl.BlockSpec((lc, LANES), lambda h, i: (0, h))
    in_specs = [pl.BlockSpec((tq, gw), lambda h, i: (i, h)), ctx_spec, ctx_spec]
    args = [q, kc, vc]
    n_lat = 0
    if k_lat is not None:
        n_lat = k_lat.shape[0] // tk
        lat_spec = pl.BlockSpec((k_lat.shape[0], LANES), lambda h, i: (0, h))
        in_specs += [lat_spec, lat_spec]
        args += [k_lat, v_lat]
    r = group * tq
    est = (4 * k_lat.shape[0] * LANES * 2 * 2 if n_lat else 0) + 6 * max(tk, lc) * r * 4 + (8 << 20)
    return pl.pallas_call(
        functools.partial(_attn_kernel, tq=tq, tk=tk, n_lat=n_lat, group=group),
        grid=(n_kv, nq // tq),
        in_specs=in_specs,
        out_specs=pl.BlockSpec((tq, gw), lambda h, i: (i, h)),
        out_shape=jax.ShapeDtypeStruct((nq, n_kv * gw), BF16),
        compiler_params=_params(est, 2),
        name="attention",
    )(*args)


def _rope_tables(n, identity):
    if identity:
        return jnp.ones((n, LANES), F32), jnp.zeros((n, LANES), F32), jnp.zeros((n, LANES), F32)
    rows = n // GRID_W
    row = jnp.repeat(jnp.arange(rows, dtype=F32), GRID_W)
    col = jnp.tile(jnp.arange(GRID_W, dtype=F32), rows)
    half = LANES // 2
    inv_freq = ROPE_THETA ** (-jnp.arange(0, half, 2, dtype=F32) / half)
    ang = jnp.concatenate([row[:, None] * inv_freq, col[:, None] * inv_freq], axis=-1)
    cos, sin = jnp.cos(ang), jnp.sin(ang)
    cos_i = jnp.repeat(cos, 2, axis=-1)
    sin_i = jnp.repeat(sin, 2, axis=-1)
    even = (jnp.arange(LANES) % 2 == 0)[None, :]
    return cos_i, jnp.where(even, -sin_i, 0.0), jnp.where(even, 0.0, sin_i)


def _dft_tables(n):
    idx = jnp.arange(n, dtype=jnp.int32)
    jk = (idx[:, None] * idx[None, :]) % n
    ang = jk.astype(F32) * (2.0 * math.pi / n)
    return jnp.cos(ang), jnp.sin(ang)


def _row_tile(rows):
    return _pick(rows, (1024, 512, 256, 128))


def _residual(lhs, w, layer, x, gate, name):
    rows, k_dim = lhs.shape
    d = x.shape[1]
    tn = _pick(d, (512, 256, 128))
    tm = _pick(rows, (512, 256, 128)) if k_dim > 4096 else _row_tile(rows)
    (out,) = _mm(
        lhs, w, layer=layer, seg_offs=[0], n_cols=d, tn=tn, tm=tm, epilogue=_epi_residual,
        extras=[(x, (tm, tn), lambda j, i: (i, j)), (gate.reshape(1, d), (1, tn), lambda j, i: (0, j))],
        outs=[(jax.ShapeDtypeStruct((rows, d), F32), (tm, tn), lambda j, i: (i, j))],
        name=name)
    return out


def _ffn(x, h2, w_up, conv_w, conv_b, w_down, layer, gate):
    rows = x.shape[0]
    d_ff = w_down.shape[1]
    tn = _pick(d_ff, (256, 128))
    tm = _row_tile(rows)
    nb = d_ff // tn
    cw = conv_w[layer]
    cb = conv_b[layer].reshape(1, 2 * d_ff)
    (act,) = _mm(
        h2, w_up, layer=layer, seg_offs=[0, d_ff], n_cols=d_ff, tn=tn, tm=tm, halo=True,
        epilogue=functools.partial(_epi_ffn_up, tm=tm),
        extras=[(cw, (3, tn), lambda j, i: (0, j)), (cw, (3, tn), lambda j, i: (0, nb + j)),
                (cb, (1, tn), lambda j, i: (0, j)), (cb, (1, tn), lambda j, i: (0, nb + j))],
        outs=[(jax.ShapeDtypeStruct((rows, d_ff), BF16), (tm, tn), lambda j, i: (i, j))],
        name="ffn_up")
    return _residual(act, w_down, layer, x, gate, "ffn_down")


def _even_in(h, w_in, e, q_g, k_g, gn_g, w_s, bs_b, tabs, dims, *, want_q, want_gmlp):
    a_q_w, a_kv_w, b_w = dims
    rows = h.shape[0]
    tm = _row_tile(rows)
    tn = _pick(a_kv_w, (512, 256, 128))
    cos, sin_e, sin_o = tabs
    tab_specs = [(t, (tm, LANES), lambda j, i: (i, 0)) for t in (cos, sin_e, sin_o)]
    q = gm = None

    def proj(off, n_cols, epilogue, extras, name):
        (out,) = _mm(
            h, w_in, layer=e, seg_offs=[off], n_cols=n_cols, tn=tn, tm=tm, epilogue=epilogue,
            extras=extras,
            outs=[(jax.ShapeDtypeStruct((rows, n_cols), BF16), (tm, tn), lambda j, i: (i, j))],
            name=name)
        return out

    if want_q:
        qscale = float(LANES ** -0.5 * math.log2(math.e))
        q = proj(0, a_q_w, functools.partial(_epi_qk, tn=tn, scale=qscale),
                 [(q_g.reshape(1, LANES), (1, LANES), lambda j, i: (0, 0))] + tab_specs, "q_proj")
    k = proj(a_q_w, a_kv_w, functools.partial(_epi_qk, tn=tn, scale=1.0),
             [(k_g.reshape(1, LANES), (1, LANES), lambda j, i: (0, 0))] + tab_specs, "k_proj")
    v = proj(a_q_w + a_kv_w, a_kv_w, _epi_cast, [], "v_proj")
    if want_gmlp:
        tg = _pick(b_w, (256, 128))
        ng = tg // LANES
        (gm,) = _mm(
            h, w_in, layer=e, seg_offs=[a_q_w + 2 * a_kv_w, a_q_w + 2 * a_kv_w + b_w], n_cols=b_w,
            tn=tg, tm=tm, epilogue=functools.partial(_epi_gmlp, tm=tm, tn=tg),
            extras=[(gn_g.reshape(1, b_w), (1, tg), lambda j, i: (0, j)),
                    (w_s, (ng, LANES, LANES), lambda j, i: (j, 0, 0)),
                    (bs_b, (ng, LANES, LANES), lambda j, i: (j, 0, 0))],
            outs=[(jax.ShapeDtypeStruct((rows, b_w), BF16), (tm, tg), lambda j, i: (i, j))],
            name="gmlp")
    return q, k, v, gm


def _fourier(f_ab, n_rows, d_w, dft_lhs):
    tm = _pick(n_rows, (256, 128))
    tn = _pick(d_w, (256, 128))
    scale = float(1.0 / math.sqrt(n_rows * LANES))
    (out,) = _mm(
        dft_lhs, f_ab.reshape(2 * n_rows, d_w), layer=0, seg_offs=[0], n_cols=d_w, tn=tn, tm=tm,
        epilogue=functools.partial(_epi_scale, scale=scale), extras=[],
        outs=[(jax.ShapeDtypeStruct((n_rows, d_w), BF16), (tm, tn), lambda j, i: (i, j))],
        name="pos_dft")
    return out


def _odd_mixer(h, w_in, o, conv_w, c_w, d_w, chan_cs, dft_lhs):
    rows = h.shape[0]
    tn = _pick(c_w, (256, 128))
    tm = _pick(rows, (512, 256, 128))
    (y_c,) = _mm(
        h, w_in, layer=o, seg_offs=[0, c_w, 2 * c_w], n_cols=c_w, tn=tn, tm=tm, halo=True,
        epilogue=functools.partial(_epi_odd_conv, tm=tm),
        extras=[(conv_w[o], (3, tn), lambda j, i: (0, j))],
        outs=[(jax.ShapeDtypeStruct((rows, c_w), BF16), (tm, tn), lambda j, i: (i, j))],
        name="odd_conv")
    tf = _pick(d_w, (512, 256, 128))
    tmf = _row_tile(rows)
    (f_ab,) = _mm(
        h, w_in, layer=o, seg_offs=[3 * c_w], n_cols=d_w, tn=tf, tm=tmf,
        epilogue=functools.partial(_epi_chan_dft, tn=tf),
        extras=[(chan_cs, (LANES, 2 * LANES), lambda j, i: (0, 0))],
        outs=[(jax.ShapeDtypeStruct((2, rows, d_w), BF16), (2, tmf, tf), lambda j, i: (0, i, j))],
        name="chan_dft")
    four = _fourier(f_ab, rows, d_w, dft_lhs)
    return jnp.concatenate([y_c, four], axis=1)


def kernel(x, c, ctx, c_ctx, w_ada, b_ada, norm1_g, norm2_g, w_ffn_up, ffn_conv_w, ffn_conv_b,
           w_ffn_down, w_in_a, w_out_a, q_norm_g, k_norm_g, gmlp_norm_g, gmlp_w_s, gmlp_b_s,
           w_in_c, w_out_c, conv_w_c):
    assert x.shape[0] == 1 and c.shape[0] == 1 and ctx.shape[0] == 1
    n, d = x.shape[1], x.shape[2]
    lc = ctx.shape[1]
    depth = w_ada.shape[0]
    assert q_norm_g.shape[-1] == LANES and gmlp_w_s.shape[2] == LANES
    b_w = gmlp_norm_g.shape[-1]
    a_q_w = w_out_a.shape[1] - b_w
    a_kv_w = (w_in_a.shape[-1] - a_q_w - 2 * b_w) // 2
    n_kv = a_kv_w // LANES
    group = a_q_w // a_kv_w
    c_w = conv_w_c.shape[-1]
    d_w = w_in_c.shape[-1] - 3 * c_w
    dims = (a_q_w, a_kv_w, b_w)

    x = x[0]
    xc = ctx[0]
    s_vecs = jnp.stack([jax.nn.silu(c[0]), jax.nn.silu(c_ctx)])
    ada = _ada(s_vecs, w_ada, b_ada)

    def mod_vecs(layer, stream):
        return [ada[layer, stream, p * d:(p + 1) * d] for p in range(6)]

    tk = _pick(n, (1024, 512, 256, 128))
    tabs_lat = _rope_tables(n, identity=False)
    tabs_ctx = _rope_tables(lc, identity=True)
    bs_b = jnp.broadcast_to(gmlp_b_s[..., None], gmlp_b_s.shape + (LANES,))

    any_odd = depth > 1
    if any_odd:
        cc, sc = _dft_tables(LANES)
        chan_cs = jnp.concatenate([cc, sc], axis=1).astype(BF16)
        cn, sn = _dft_tables(n)
        dft_lat = jnp.concatenate([cn, -sn], axis=1).astype(BF16)
        cl, sl = _dft_tables(lc)
        dft_ctx = jnp.concatenate([cl, -sl], axis=1).astype(BF16)

    for i in range(depth):
        ctx_later = any(j % 2 == 0 for j in range(i + 1, depth))
        sh1, sc1, g1, sh2, sc2, g2 = mod_vecs(i, 0)
        csh1, csc1, cg1, csh2, csc2, cg2 = mod_vecs(i, 1)
        h = _modulate(x, norm1_g[i], sc1, sh1)
        if i % 2 == 0:
            e = i // 2
            hc = _modulate(xc, norm1_g[i], csc1, csh1)
            q, k, v, gm = _even_in(h, w_in_a, e, q_norm_g[e], k_norm_g[e], gmlp_norm_g[e],
                                   gmlp_w_s[e], bs_b[e], tabs_lat, dims,
                                   want_q=True, want_gmlp=True)
            qc, kc, vc, gmc = _even_in(hc, w_in_a, e, q_norm_g[e], k_norm_g[e], gmlp_norm_g[e],
                                       gmlp_w_s[e], bs_b[e], tabs_ctx, dims,
                                       want_q=ctx_later, want_gmlp=ctx_later)
            attn = _attention(q, kc, vc, k, v, n_kv=n_kv, group=group, tk=tk)
            mix = jnp.concatenate([attn, gm], axis=1)
            x = _residual(mix, w_out_a, e, x, g1, "mix_out")
            if ctx_later:
                attn_c = _attention(qc, kc, vc, None, None, n_kv=n_kv, group=group, tk=tk)
                mix_c = jnp.concatenate([attn_c, gmc], axis=1)
                xc = _residual(mix_c, w_out_a, e, xc, cg1, "mix_out")
        else:
            o = i // 2
            mix = _odd_mixer(h, w_in_c, o, conv_w_c, c_w, d_w, chan_cs, dft_lat)
            x = _residual(mix, w_out_c, o, x, g1, "mix_out")
            if ctx_later:
                hc = _modulate(xc, norm1_g[i], csc1, csh1)
                mix_c = _odd_mixer(hc, w_in_c, o, conv_w_c, c_w, d_w, chan_cs, dft_ctx)
                xc = _residual(mix_c, w_out_c, o, xc, cg1, "mix_out")
        h2 = _modulate(x, norm2_g[i], sc2, sh2)
        x = _ffn(x, h2, w_ffn_up, ffn_conv_w, ffn_conv_b, w_ffn_down, i, g2)
        if ctx_later:
            hc2 = _modulate(xc, norm2_g[i], csc2, csh2)
            xc = _ffn(xc, hc2, w_ffn_up, ffn_conv_w, ffn_conv_b, w_ffn_down, i, cg2)
    return x[None]
```

```python
import functools
import math

import jax
import jax.numpy as jnp
from jax import lax
from jax.experimental import pallas as pl
from jax.experimental.pallas import tpu as pltpu

BF16 = jnp.bfloat16
F32 = jnp.float32

GRID_W = 64
ROPE_THETA = 10000.0
EPS = 1e-6
LANES = 128
HALO = 16
VMEM_CAP = 60 * 1024 * 1024
NEG_BIG = -1e30


def _pick(n, prefs):
    for p in prefs:
        if n % p == 0:
            return p
    return n


def _params(vmem_bytes, n_axes):
    return pltpu.CompilerParams(
        dimension_semantics=("arbitrary",) * n_axes,
        vmem_limit_bytes=int(min(VMEM_CAP, vmem_bytes)),
    )


def _ada_kernel(sb_ref, w_ref, b_ref, o_ref, *, tn, n_vec):
    for t in range(tn // LANES):
        sl = slice(t * LANES, (t + 1) * LANES)
        w = w_ref[:, sl]
        for r in range(n_vec):
            o_ref[r:r + 1, sl] = jnp.sum(w * sb_ref[r], axis=0, keepdims=True) + b_ref[:, sl]


def _ada(s_vecs, w_ada, b_ada):
    depth, d, n6 = w_ada.shape
    n_vec = s_vecs.shape[0]
    tn = _pick(n6, (512, 256, 128))
    sb = jnp.broadcast_to(s_vecs[:, :, None], (n_vec, d, LANES))
    return pl.pallas_call(
        functools.partial(_ada_kernel, tn=tn, n_vec=n_vec),
        grid=(depth, n6 // tn),
        in_specs=[
            pl.BlockSpec((n_vec, d, LANES), lambda l, j: (0, 0, 0)),
            pl.BlockSpec((None, d, tn), lambda l, j: (l, 0, j)),
            pl.BlockSpec((None, 1, tn), lambda l, j: (l, 0, j)),
        ],
        out_specs=pl.BlockSpec((None, n_vec, tn), lambda l, j: (l, 0, j)),
        out_shape=jax.ShapeDtypeStruct((depth, n_vec, n6), F32),
        compiler_params=_params(2 * d * tn * 4 + 4 * n_vec * d * LANES * 4 + (8 << 20), 2),
        name="ada",
    )(sb, w_ada, b_ada.reshape(depth, 1, n6))


def _modulate_kernel(x_ref, g_ref, sc_ref, sh_ref, o_ref):
    x = x_ref[...]
    ms = jnp.mean(x * x, axis=-1, keepdims=True)
    y = x * lax.rsqrt(ms + EPS)
    o_ref[...] = ((y * g_ref[...]) * (1.0 + sc_ref[...]) + sh_ref[...]).astype(BF16)


def _modulate(x, g, scale, shift):
    rows, d = x.shape
    tm = _pick(rows, (256, 128, 64, 32, 16))
    vec = pl.BlockSpec((1, d), lambda i: (0, 0))
    return pl.pallas_call(
        _modulate_kernel,
        grid=(rows // tm,),
        in_specs=[pl.BlockSpec((tm, d), lambda i: (i, 0)), vec, vec, vec],
        out_specs=pl.BlockSpec((tm, d), lambda i: (i, 0)),
        out_shape=jax.ShapeDtypeStruct((rows, d), BF16),
        compiler_params=_params(6 * tm * d * 4 + (8 << 20), 1),
        name="modulate",
    )(x, g.reshape(1, d), scale.reshape(1, d), shift.reshape(1, d))


def _modulate_halo_kernel(x_ref, above_ref, below_ref, g_ref, sc_ref, sh_ref, o_ref, *, tile, sub):
    t, s = pl.program_id(0), pl.program_id(1)
    n_t, n_s = pl.num_programs(0), pl.num_programs(1)
    a = g_ref[...] * (1.0 + sc_ref[...])
    b = sh_ref[...]

    def mod(x):
        ms = jnp.mean(x * x, axis=-1, keepdims=True)
        return ((x * lax.rsqrt(ms + EPS)) * a + b).astype(BF16)

    o_ref[pl.ds(pl.multiple_of(s * sub, sub), sub), :] = mod(x_ref[...])

    @pl.when(s == n_s - 1)
    def _():
        zero = jnp.zeros((HALO, x_ref.shape[1]), BF16)
        o_ref[tile:tile + HALO, :] = jnp.where(t < n_t - 1, mod(below_ref[...]), zero)
        o_ref[tile + HALO:tile + 2 * HALO, :] = jnp.where(t > 0, mod(above_ref[...]), zero)


def _modulate_halo(x, g, scale, shift, tile):
    rows, d = x.shape
    assert rows % tile == 0 and tile % HALO == 0
    sub = _pick(tile, (256, 128, 64, 32, 16))
    n_t, n_s = rows // tile, tile // sub
    hb, nblk = tile // HALO, rows // HALO
    vec = pl.BlockSpec((1, d), lambda t, s: (0, 0))
    return pl.pallas_call(
        functools.partial(_modulate_halo_kernel, tile=tile, sub=sub),
        grid=(n_t, n_s),
        in_specs=[
            pl.BlockSpec((sub, d), lambda t, s: (t * n_s + s, 0)),
            pl.BlockSpec((HALO, d), lambda t, s: (jnp.maximum(t * hb - 1, 0), 0)),
            pl.BlockSpec((HALO, d), lambda t, s: (jnp.minimum((t + 1) * hb, nblk - 1), 0)),
            vec, vec, vec,
        ],
        out_specs=pl.BlockSpec((None, tile + 2 * HALO, d), lambda t, s: (t, 0, 0)),
        out_shape=jax.ShapeDtypeStruct((n_t, tile + 2 * HALO, d), BF16),
        compiler_params=_params(4 * sub * d * 4 + 2 * (tile + 2 * HALO) * d * 2 + (8 << 20), 2),
        name="modulate_halo",
    )(x, x, x, g.reshape(1, d), scale.reshape(1, d), shift.reshape(1, d))


def _mm_kernel(*refs, n_seg, k_parts, halo, cast_w, lag, n_tiles, tm, epilogue, n_extra, n_out,
               k_dim):
    it = iter(refs)
    lhs_refs = [next(it) for _ in k_parts]
    w_refs = [next(it) for _ in range(n_seg)]
    extra_refs = [next(it) for _ in range(n_extra)]
    out_refs = [next(it) for _ in range(n_out)]
    wb_refs = [next(it) for _ in range(n_seg)] if cast_w else w_refs
    n_slots = 2 if lag else (1 if halo else 0)
    acc_slots = [[next(it) for _ in range(n_seg)] for _ in range(n_slots)]

    i = pl.program_id(1)
    if cast_w:
        ck = _pick(k_dim, (256, 128, 64, 32, 16))

        @pl.when(i == 0)
        def _():
            def cast_chunk(c, carry):
                rows = pl.ds(pl.multiple_of(c * ck, ck), ck)
                for w_ref, wb_ref in zip(w_refs, wb_refs):
                    wb_ref[rows, :] = w_ref[rows, :].astype(BF16)
                return carry
            lax.fori_loop(0, k_dim // ck, cast_chunk, 0)

    def dots(between=()):
        lhs_vals = [r[...] for r in lhs_refs]
        if between:
            kc = k_dim // len(between)
            accs = [None] * n_seg
            for c, epi_chunk in enumerate(between):
                for s, wb_ref in enumerate(wb_refs):
                    part = jnp.dot(lhs_vals[0][:, c * kc:(c + 1) * kc], wb_ref[c * kc:(c + 1) * kc, :],
                                   preferred_element_type=F32)
                    accs[s] = part if accs[s] is None else accs[s] + part
                epi_chunk()
            return accs
        accs = []
        for wb_ref in wb_refs:
            acc, k0 = None, 0
            for lhs, kp in zip(lhs_vals, k_parts):
                part = jnp.dot(lhs, wb_ref[k0:k0 + kp, :], preferred_element_type=F32)
                acc = part if acc is None else acc + part
                k0 += kp
            accs.append(acc)
        return accs

    def store(slot, between=()):
        for acc, ref in zip(dots(between), acc_slots[slot]):
            if halo:
                ref[HALO:, :] = acc[:tm + HALO, :]
                ref[:HALO, :] = acc[tm + HALO:, :]
            else:
                ref[...] = acc

    def finish(slot):
        accs = acc_slots[slot] if halo else [ref[...] for ref in acc_slots[slot]]
        epilogue(accs, extra_refs, out_refs)

    if not lag:
        if halo:
            store(0)
            finish(0)
        else:
            epilogue(dots(), extra_refs, out_refs)
        return


    @pl.when(i == 0)
    def _():
        store(0)

    for parity in range(2):
        @pl.when(jnp.logical_and(jnp.logical_and(i > 0, i < n_tiles), i % 2 == parity))
        def _():
            if halo:
                nr = min(CONV_ROWS, tm)
                store(parity, [functools.partial(epilogue, acc_slots[1 - parity], extra_refs, out_refs,
                                                 rows=(r0, nr)) for r0 in range(0, tm, nr)])
            else:
                finish(1 - parity)
                store(parity)

    @pl.when(i == n_tiles)
    def _():
        finish((n_tiles - 1) % 2)


def _mm(lhs, w, *, layer, seg_offs, n_cols, tn, tm, epilogue, extras, outs, halo=False, lag=False,
        name="mm"):
    lhs_parts = list(lhs) if isinstance(lhs, (list, tuple)) else [lhs]
    tiled = lhs_parts[0].ndim == 3
    if tiled:
        assert len(lhs_parts) == 1 and lhs_parts[0].shape[1] == tm + 2 * HALO
        rows = lhs_parts[0].shape[0] * tm
        k_parts = (lhs_parts[0].shape[2],)
    else:
        assert not halo
        rows = lhs_parts[0].shape[0]
        k_parts = tuple(p.shape[1] for p in lhs_parts)
    k_dim = sum(k_parts)
    assert rows % tm == 0 and n_cols % tn == 0
    n_tiles = rows // tm
    cast_w = w.dtype != BF16
    if lag:
        def dot_tile(i):
            return jnp.minimum(i, n_tiles - 1)

        def epi_tile(i):
            return jnp.maximum(i - 1, 0)
    else:
        def dot_tile(i):
            return i
        epi_tile = dot_tile
    if tiled:
        in_specs = [pl.BlockSpec((None, tm + 2 * HALO if halo else tm, k_dim),
                                 lambda j, i: (dot_tile(i), 0, 0))]
    else:
        in_specs = [pl.BlockSpec((tm, kp), lambda j, i: (dot_tile(i), 0)) for kp in k_parts]
    args = list(lhs_parts)
    for off in seg_offs:
        assert off % tn == 0
        ob = off // tn
        if w.ndim == 3:
            in_specs.append(pl.BlockSpec((None, k_dim, tn), lambda j, i, ob=ob: (layer, 0, ob + j)))
        else:
            in_specs.append(pl.BlockSpec((k_dim, tn), lambda j, i, ob=ob: (0, ob + j)))
        args.append(w)
    for arr, bs, im in extras:
        in_specs.append(pl.BlockSpec(bs, lambda j, i, im=im: im(j, epi_tile(i))))
        args.append(arr)
    out_specs = [pl.BlockSpec(bs, lambda j, i, im=im: im(j, epi_tile(i))) for _, bs, im in outs]
    out_shape = [sds for sds, _, _ in outs]
    n_seg = len(seg_offs)
    rows_l = tm + 2 * HALO if halo else tm
    scratch = []
    if cast_w:
        scratch += [pltpu.VMEM((k_dim, tn), BF16) for _ in range(n_seg)]
    n_slots = 2 if lag else (1 if halo else 0)
    scratch += [pltpu.VMEM((rows_l, tn), F32) for _ in range(n_slots * n_seg)]
    wbytes = w.dtype.itemsize
    est = (n_seg * k_dim * tn * (2 * wbytes + (2 if cast_w else 0))
           + 2 * rows_l * k_dim * 2 + (5 + 4 * n_seg) * rows_l * tn * 4 + (8 << 20))
    kern = functools.partial(
        _mm_kernel, n_seg=n_seg, k_parts=k_parts, halo=halo, cast_w=cast_w, lag=lag, n_tiles=n_tiles,
        tm=tm, epilogue=epilogue, n_extra=len(extras), n_out=len(outs), k_dim=k_dim)
    return pl.pallas_call(
        kern,
        grid=(n_cols // tn, n_tiles + (1 if lag else 0)),
        in_specs=in_specs,
        out_specs=out_specs,
        out_shape=out_shape,
        scratch_shapes=scratch,
        compiler_params=_params(est, 2),
        name=name,
    )(*args)


def _rms_lanes(x):
    return x * lax.rsqrt(jnp.mean(x * x, axis=-1, keepdims=True) + EPS)


def _epi_qk(accs, ex, outs, *, tn, scale):
    (acc,) = accs
    g_ref, cos_ref, sin_e_ref, sin_o_ref = ex
    (o_ref,) = outs
    g = g_ref[...]
    cos, sin_e, sin_o = cos_ref[...], sin_e_ref[...], sin_o_ref[...]
    for h in range(tn // LANES):
        hs = slice(h * LANES, (h + 1) * LANES)
        y = _rms_lanes(acc[:, hs]) * g
        r = y * cos + pltpu.roll(y, LANES - 1, 1) * sin_e + pltpu.roll(y, 1, 1) * sin_o
        o_ref[:, hs] = (r * scale if scale != 1.0 else r).astype(BF16)


def _epi_cast(accs, ex, outs):
    (acc,) = accs
    (o_ref,) = outs
    o_ref[...] = acc.astype(BF16)


def _epi_gmlp(accs, ex, outs, *, tm, tn):
    u, gv = accs
    gn_ref, ws_ref, bs_ref = ex
    (o_ref,) = outs
    gu = jax.nn.gelu(u)
    ggv = jax.nn.gelu(gv)
    nc = tm // LANES
    for g in range(tn // LANES):
        gs = slice(g * LANES, (g + 1) * LANES)
        vn = (_rms_lanes(ggv[:, gs]) * gn_ref[:, gs]).astype(BF16)
        ws = ws_ref[g].astype(BF16)
        bias = bs_ref[g]
        for c in range(nc):
            cs = slice(c * LANES, (c + 1) * LANES)
            res = jnp.dot(ws, vn[cs, :], preferred_element_type=F32)
            o_ref[cs, gs] = (gu[cs, gs] * (res + bias)).astype(BF16)


CONV_ROWS = 64


def _row_windows(ref, r0, nr):
    base = HALO + r0
    return ref[base - 1:base - 1 + nr, :], ref[base:base + nr, :], ref[base + 1:base + 1 + nr, :]


def _conv_chunks(tm, rows):
    nr = min(CONV_ROWS, tm)
    return [rows] if rows is not None else [(r0, nr) for r0 in range(0, tm, nr)]


def _epi_ffn_up(acc_refs, ex, outs, *, tm, rows=None):
    g_ref, u_ref = acc_refs
    cwg_ref, cwu_ref, cbg_ref, cbu_ref = ex
    (o_ref,) = outs
    for r0, nr in _conv_chunks(tm, rows):
        gp, gm, gn = _row_windows(g_ref, r0, nr)
        up, um, un = _row_windows(u_ref, r0, nr)
        ag = gp * cwg_ref[0:1, :] + gm * cwg_ref[1:2, :] + gn * cwg_ref[2:3, :] + cbg_ref[...]
        au = up * cwu_ref[0:1, :] + um * cwu_ref[1:2, :] + un * cwu_ref[2:3, :] + cbu_ref[...]
        o_ref[r0:r0 + nr, :] = (jax.nn.silu(ag) * au).astype(BF16)


def _epi_odd_conv(acc_refs, ex, outs, *, tm, rows=None):
    bg_ref, cg_ref, hx_ref = acc_refs
    (cw_ref,) = ex
    (o_ref,) = outs
    for r0, nr in _conv_chunks(tm, rows):
        cp, cm, cn = _row_windows(cg_ref, r0, nr)
        hp, hm, hn = _row_windows(hx_ref, r0, nr)
        conv = (cp * hp) * cw_ref[0:1, :] + (cm * hm) * cw_ref[1:2, :] + (cn * hn) * cw_ref[2:3, :]
        o_ref[r0:r0 + nr, :] = (bg_ref[HALO + r0:HALO + r0 + nr, :] * conv).astype(BF16)


def _epi_residual(accs, ex, outs):
    (acc,) = accs
    x_ref, gate_ref = ex
    (o_ref,) = outs
    o_ref[...] = x_ref[...] + gate_ref[...] * acc


def _epi_chan_dft(accs, ex, outs, *, tn):
    (acc,) = accs
    (cs_ref,) = ex
    (o_ref,) = outs
    cs = cs_ref[...]
    for g in range(tn // LANES):
        gs = slice(g * LANES, (g + 1) * LANES)
        ab = jnp.dot(acc[:, gs].astype(BF16), cs, preferred_element_type=F32)
        o_ref[0, :, gs] = ab[:, :LANES].astype(BF16)
        o_ref[1, :, gs] = ab[:, LANES:].astype(BF16)


def _epi_scale(accs, ex, outs, *, scale):
    (acc,) = accs
    (o_ref,) = outs
    o_ref[...] = (acc * scale).astype(BF16)


_NT = (((1,), (1,)), ((), ()))
_TN = (((0,), (0,)), ((), ()))


def _attn_kernel(*refs, tq, tk, n_lat, group):
    if n_lat:
        q_ref, kc_ref, vc_ref, k_ref, v_ref, o_ref = refs
    else:
        q_ref, kc_ref, vc_ref, o_ref = refs
    r = group * tq
    q = jnp.concatenate([q_ref[:, g * LANES:(g + 1) * LANES] for g in range(group)], axis=0)

    def step(k_t, v_t, carry):
        m, l, acc = carry
        s = lax.dot_general(k_t, q, _NT, preferred_element_type=F32)
        m_new = jnp.maximum(m, jnp.max(s, axis=0, keepdims=True))
        alpha = jnp.exp2(m - m_new)
        p = jnp.exp2(s - m_new)
        l = alpha * l + jnp.sum(p, axis=0, keepdims=True)
        pv = lax.dot_general(v_t, p.astype(BF16), _TN, preferred_element_type=F32)
        return m_new, l, alpha * acc + pv

    carry = (jnp.full((1, r), NEG_BIG, F32), jnp.zeros((1, r), F32), jnp.zeros((LANES, r), F32))
    carry = step(kc_ref[...], vc_ref[...], carry)
    if n_lat:
        def body(t, c):
            rows = pl.ds(pl.multiple_of(t * tk, tk), tk)
            return step(k_ref[rows, :], v_ref[rows, :], c)
        carry = lax.fori_loop(0, n_lat, body, carry, unroll=True)
    _, l, acc = carry
    out_t = (acc * (1.0 / l)).T
    for g in range(group):
        o_ref[:, g * LANES:(g + 1) * LANES] = out_t[g * tq:(g + 1) * tq, :].astype(BF16)


def _attention(q, kc, vc, k_lat, v_lat, *, n_kv, group, tk):
    nq = q.shape[0]
    lc = kc.shape[0]
    tq = _pick(nq, (256, 128))
    gw = group * LANES
    ctx_spec = pl.BlockSpec((lc, LANES), lambda h, i: (0, h))
    in_specs = [pl.BlockSpec((tq, gw), lambda h, i: (i, h)), ctx_spec, ctx_spec]
    args = [q, kc, vc]
    n_lat = 0
    if k_lat is not None:
        n_lat = k_lat.shape[0] // tk
        lat_spec = pl.BlockSpec((k_lat.shape[0], LANES), lambda h, i: (0, h))
        in_specs += [lat_spec, lat_spec]
        args += [k_lat, v_lat]
    r = group * tq
    est = (4 * k_lat.shape[0] * LANES * 2 * 2 if n_lat else 0) + 6 * max(tk, lc) * r * 4 + (8 << 20)
    return pl.pallas_call(
        functools.partial(_attn_kernel, tq=tq, tk=tk, n_lat=n_lat, group=group),
        grid=(n_kv, nq // tq),
        in_specs=in_specs,
        out_specs=pl.BlockSpec((tq, gw), lambda h, i: (i, h)),
        out_shape=jax.ShapeDtypeStruct((nq, n_kv * gw), BF16),
        compiler_params=_params(est, 2),
        name="attention",
    )(*args)


def _rope_tables(n, identity):
    if identity:
        return jnp.ones((n, LANES), F32), jnp.zeros((n, LANES), F32), jnp.zeros((n, LANES), F32)
    rows = n // GRID_W
    row = jnp.repeat(jnp.arange(rows, dtype=F32), GRID_W)
    col = jnp.tile(jnp.arange(GRID_W, dtype=F32), rows)
    half = LANES // 2
    inv_freq = ROPE_THETA ** (-jnp.arange(0, half, 2, dtype=F32) / half)
    ang = jnp.concatenate([row[:, None] * inv_freq, col[:, None] * inv_freq], axis=-1)
    cos, sin = jnp.cos(ang), jnp.sin(ang)
    cos_i = jnp.repeat(cos, 2, axis=-1)
    sin_i = jnp.repeat(sin, 2, axis=-1)
    even = (jnp.arange(LANES) % 2 == 0)[None, :]
    return cos_i, jnp.where(even, -sin_i, 0.0), jnp.where(even, 0.0, sin_i)


def _dft_tables(n):
    idx = jnp.arange(n, dtype=jnp.int32)
    jk = (idx[:, None] * idx[None, :]) % n
    ang = jk.astype(F32) * (2.0 * math.pi / n)
    return jnp.cos(ang), jnp.sin(ang)


def _dft_lhs(n):
    n0 = LANES if n % LANES == 0 and n > LANES else 1
    n1 = n // n0
    kk = jnp.arange(2 * n, dtype=jnp.int32)
    k = kk % n
    phase = jnp.where(kk >= n, 0.5 * math.pi, 0.0).astype(F32)
    j1 = jnp.arange(n1, dtype=jnp.int32)[:, None]
    j0 = jnp.arange(n0, dtype=jnp.int32)[:, None]
    coarse = ((j1 * k[None, :]) % n1).astype(F32) * (2.0 * math.pi / n1)
    fine = ((j0 * k[None, :]) % n).astype(F32) * (2.0 * math.pi / n) + phase
    ca, sa = jnp.cos(coarse)[:, None, :], jnp.sin(coarse)[:, None, :]
    cb, sb = jnp.cos(fine)[None, :, :], jnp.sin(fine)[None, :, :]
    return (ca * cb - sa * sb).astype(BF16).reshape(n, 2 * n)


def _row_tile(rows):
    return _pick(rows, (1024, 512, 256, 128))


def _residual(lhs, w, layer, x, gate, name):
    parts = list(lhs) if isinstance(lhs, (list, tuple)) else [lhs]
    rows, k_dim = parts[0].shape[0], sum(p.shape[1] for p in parts)
    d = x.shape[1]
    tn = _pick(d, (512, 256, 128))
    tm = _pick(rows, (512, 256, 128)) if k_dim > 4096 else _row_tile(rows)
    (out,) = _mm(
        lhs, w, layer=layer, seg_offs=[0], n_cols=d, tn=tn, tm=tm, epilogue=_epi_residual,
        extras=[(x, (tm, tn), lambda j, i: (i, j)), (gate.reshape(1, d), (1, tn), lambda j, i: (0, j))],
        outs=[(jax.ShapeDtypeStruct((rows, d), F32), (tm, tn), lambda j, i: (i, j))],
        name=name)
    return out


def _ffn(x, mod, w_up, conv_w, conv_b, w_down, layer, gate):
    rows = x.shape[0]
    d_ff = w_down.shape[1]
    tn = _pick(d_ff, (256, 128))
    tm = _row_tile(rows)
    nb = d_ff // tn
    h2 = _modulate_halo(x, *mod, tile=tm)
    cw = conv_w[layer]
    cb = conv_b[layer].reshape(1, 2 * d_ff)
    (act,) = _mm(
        h2, w_up, layer=layer, seg_offs=[0, d_ff], n_cols=d_ff, tn=tn, tm=tm, halo=True,
        lag=rows > tm, epilogue=functools.partial(_epi_ffn_up, tm=tm),
        extras=[(cw, (3, tn), lambda j, i: (0, j)), (cw, (3, tn), lambda j, i: (0, nb + j)),
                (cb, (1, tn), lambda j, i: (0, j)), (cb, (1, tn), lambda j, i: (0, nb + j))],
        outs=[(jax.ShapeDtypeStruct((rows, d_ff), BF16), (tm, tn), lambda j, i: (i, j))],
        name="ffn_up")
    return _residual(act, w_down, layer, x, gate, "ffn_down")


def _even_in(h, w_in, e, q_g, k_g, gn_g, w_s, bs_b, tabs, dims, *, want_q, want_gmlp):
    a_q_w, a_kv_w, b_w = dims
    rows = h.shape[0]
    tm = _row_tile(rows)
    tn = _pick(a_kv_w, (512, 256, 128))
    cos, sin_e, sin_o = tabs
    tab_specs = [(t, (tm, LANES), lambda j, i: (i, 0)) for t in (cos, sin_e, sin_o)]
    q = gm = None

    def proj(off, n_cols, epilogue, extras, name, lag=False):
        (out,) = _mm(
            h, w_in, layer=e, seg_offs=[off], n_cols=n_cols, tn=tn, tm=tm, epilogue=epilogue,
            extras=extras, lag=lag and rows > tm,
            outs=[(jax.ShapeDtypeStruct((rows, n_cols), BF16), (tm, tn), lambda j, i: (i, j))],
            name=name)
        return out

    if want_q:
        qscale = float(LANES ** -0.5 * math.log2(math.e))
        q = proj(0, a_q_w, functools.partial(_epi_qk, tn=tn, scale=qscale),
                 [(q_g.reshape(1, LANES), (1, LANES), lambda j, i: (0, 0))] + tab_specs, "q_proj",
                 lag=True)
    k = proj(a_q_w, a_kv_w, functools.partial(_epi_qk, tn=tn, scale=1.0),
             [(k_g.reshape(1, LANES), (1, LANES), lambda j, i: (0, 0))] + tab_specs, "k_proj",
             lag=True)
    v = proj(a_q_w + a_kv_w, a_kv_w, _epi_cast, [], "v_proj")
    if want_gmlp:
        tg = _pick(b_w, (256, 128))
        ng = tg // LANES
        (gm,) = _mm(
            h, w_in, layer=e, seg_offs=[a_q_w + 2 * a_kv_w, a_q_w + 2 * a_kv_w + b_w], n_cols=b_w,
            tn=tg, tm=tm, lag=rows > tm, epilogue=functools.partial(_epi_gmlp, tm=tm, tn=tg),
            extras=[(gn_g.reshape(1, b_w), (1, tg), lambda j, i: (0, j)),
                    (w_s, (ng, LANES, LANES), lambda j, i: (j, 0, 0)),
                    (bs_b, (ng, LANES, LANES), lambda j, i: (j, 0, 0))],
            outs=[(jax.ShapeDtypeStruct((rows, b_w), BF16), (tm, tg), lambda j, i: (i, j))],
            name="gmlp")
    return q, k, v, gm


def _fourier(f_ab, n_rows, d_w, dft_lhs):
    tm = _pick(n_rows, (256, 128))
    tn = _pick(d_w, (256, 128))
    scale = float(1.0 / math.sqrt(n_rows * LANES))
    (out,) = _mm(
        dft_lhs, f_ab.reshape(2 * n_rows, d_w), layer=0, seg_offs=[0], n_cols=d_w, tn=tn, tm=tm,
        epilogue=functools.partial(_epi_scale, scale=scale), extras=[],
        outs=[(jax.ShapeDtypeStruct((n_rows, d_w), BF16), (tm, tn), lambda j, i: (i, j))],
        name="pos_dft")
    return out


def _odd_mixer(x, mod, w_in, o, conv_w, c_w, d_w, chan_cs, dft_lhs):
    rows = x.shape[0]
    tn = _pick(c_w, (256, 128))
    tm = _pick(rows, (512, 256, 128))
    h = _modulate_halo(x, *mod, tile=tm)
    (y_c,) = _mm(
        h, w_in, layer=o, seg_offs=[0, c_w, 2 * c_w], n_cols=c_w, tn=tn, tm=tm, halo=True,
        lag=rows > tm, epilogue=functools.partial(_epi_odd_conv, tm=tm),
        extras=[(conv_w[o], (3, tn), lambda j, i: (0, j))],
        outs=[(jax.ShapeDtypeStruct((rows, c_w), BF16), (tm, tn), lambda j, i: (i, j))],
        name="odd_conv")
    tf = _pick(d_w, (512, 256, 128))
    tmf = tm
    (f_ab,) = _mm(
        h, w_in, layer=o, seg_offs=[3 * c_w], n_cols=d_w, tn=tf, tm=tmf,
        lag=rows > tmf, epilogue=functools.partial(_epi_chan_dft, tn=tf),
        extras=[(chan_cs, (LANES, 2 * LANES), lambda j, i: (0, 0))],
        outs=[(jax.ShapeDtypeStruct((2, rows, d_w), BF16), (2, tmf, tf), lambda j, i: (0, i, j))],
        name="chan_dft")
    four = _fourier(f_ab, rows, d_w, dft_lhs)
    return [y_c, four]


def kernel(x, c, ctx, c_ctx, w_ada, b_ada, norm1_g, norm2_g, w_ffn_up, ffn_conv_w, ffn_conv_b,
           w_ffn_down, w_in_a, w_out_a, q_norm_g, k_norm_g, gmlp_norm_g, gmlp_w_s, gmlp_b_s,
           w_in_c, w_out_c, conv_w_c):
    assert x.shape[0] == 1 and c.shape[0] == 1 and ctx.shape[0] == 1
    n, d = x.shape[1], x.shape[2]
    lc = ctx.shape[1]
    depth = w_ada.shape[0]
    assert q_norm_g.shape[-1] == LANES and gmlp_w_s.shape[2] == LANES
    b_w = gmlp_norm_g.shape[-1]
    a_q_w = w_out_a.shape[1] - b_w
    a_kv_w = (w_in_a.shape[-1] - a_q_w - 2 * b_w) // 2
    n_kv = a_kv_w // LANES
    group = a_q_w // a_kv_w
    c_w = conv_w_c.shape[-1]
    d_w = w_in_c.shape[-1] - 3 * c_w
    dims = (a_q_w, a_kv_w, b_w)

    x = x[0]
    xc = ctx[0]
    s_vecs = jnp.stack([jax.nn.silu(c[0]), jax.nn.silu(c_ctx)])
    ada = _ada(s_vecs, w_ada, b_ada)

    def mod_vecs(layer, stream):
        return [ada[layer, stream, p * d:(p + 1) * d] for p in range(6)]

    tk = _pick(n, (1024, 512, 256, 128))
    tabs_lat = _rope_tables(n, identity=False)
    tabs_ctx = _rope_tables(lc, identity=True)
    bs_b = jnp.broadcast_to(gmlp_b_s[..., None], gmlp_b_s.shape + (LANES,))

    any_odd = depth > 1
    if any_odd:
        cc, sc = _dft_tables(LANES)
        chan_cs = jnp.concatenate([cc, sc], axis=1).astype(BF16)
        dft_lat = _dft_lhs(n)
        dft_ctx = _dft_lhs(lc)

    for i in range(depth):
        ctx_later = any(j % 2 == 0 for j in range(i + 1, depth))
        sh1, sc1, g1, sh2, sc2, g2 = mod_vecs(i, 0)
        csh1, csc1, cg1, csh2, csc2, cg2 = mod_vecs(i, 1)
        if i % 2 == 0:
            e = i // 2
            h = _modulate(x, norm1_g[i], sc1, sh1)
            hc = _modulate(xc, norm1_g[i], csc1, csh1)
            q, k, v, gm = _even_in(h, w_in_a, e, q_norm_g[e], k_norm_g[e], gmlp_norm_g[e],
                                   gmlp_w_s[e], bs_b[e], tabs_lat, dims,
                                   want_q=True, want_gmlp=True)
            qc, kc, vc, gmc = _even_in(hc, w_in_a, e, q_norm_g[e], k_norm_g[e], gmlp_norm_g[e],
                                       gmlp_w_s[e], bs_b[e], tabs_ctx, dims,
                                       want_q=ctx_later, want_gmlp=ctx_later)
            attn = _attention(q, kc, vc, k, v, n_kv=n_kv, group=group, tk=tk)
            mix = [attn, gm]
            x = _residual(mix, w_out_a, e, x, g1, "mix_out")
            if ctx_later:
                attn_c = _attention(qc, kc, vc, None, None, n_kv=n_kv, group=group, tk=tk)
                mix_c = [attn_c, gmc]
                xc = _residual(mix_c, w_out_a, e, xc, cg1, "mix_out")
        else:
            o = i // 2
            mix = _odd_mixer(x, (norm1_g[i], sc1, sh1), w_in_c, o, conv_w_c, c_w, d_w, chan_cs, dft_lat)
            x = _residual(mix, w_out_c, o, x, g1, "mix_out")
            if ctx_later:
                mix_c = _odd_mixer(xc, (norm1_g[i], csc1, csh1), w_in_c, o, conv_w_c, c_w, d_w,
                                   chan_cs, dft_ctx)
                xc = _residual(mix_c, w_out_c, o, xc, cg1, "mix_out")
        x = _ffn(x, (norm2_g[i], sc2, sh2), w_ffn_up, ffn_conv_w, ffn_conv_b, w_ffn_down, i, g2)
        if ctx_later:
            xc = _ffn(xc, (norm2_g[i], csc2, csh2), w_ffn_up, ffn_conv_w, ffn_conv_b, w_ffn_down, i, cg2)
    return x[None]
```

```python
import functools
import math

import jax
import jax.numpy as jnp
from jax import lax
from jax.experimental import pallas as pl
from jax.experimental.pallas import tpu as pltpu

BF16 = jnp.bfloat16
F32 = jnp.float32

GRID_W = 64
ROPE_THETA = 10000.0
EPS = 1e-6
LANES = 128
HALO = 16
VMEM_CAP = 60 * 1024 * 1024
NEG_BIG = -1e30


def _pick(n, prefs):
    for p in prefs:
        if n % p == 0:
            return p
    return n


def _params(vmem_bytes, n_axes):
    return pltpu.CompilerParams(
        dimension_semantics=("arbitrary",) * n_axes,
        vmem_limit_bytes=int(min(VMEM_CAP, vmem_bytes)),
    )


def _ada_kernel(sb_ref, w_ref, b_ref, o_ref, *, tn, n_vec):
    for t in range(tn // LANES):
        sl = slice(t * LANES, (t + 1) * LANES)
        w = w_ref[:, sl]
        for r in range(n_vec):
            o_ref[r:r + 1, sl] = jnp.sum(w * sb_ref[r], axis=0, keepdims=True) + b_ref[:, sl]


def _ada(s_vecs, w_ada, b_ada):
    depth, d, n6 = w_ada.shape
    n_vec = s_vecs.shape[0]
    tn = _pick(n6, (512, 256, 128))
    sb = jnp.broadcast_to(s_vecs[:, :, None], (n_vec, d, LANES))
    return pl.pallas_call(
        functools.partial(_ada_kernel, tn=tn, n_vec=n_vec),
        grid=(depth, n6 // tn),
        in_specs=[
            pl.BlockSpec((n_vec, d, LANES), lambda l, j: (0, 0, 0)),
            pl.BlockSpec((None, d, tn), lambda l, j: (l, 0, j)),
            pl.BlockSpec((None, 1, tn), lambda l, j: (l, 0, j)),
        ],
        out_specs=pl.BlockSpec((None, n_vec, tn), lambda l, j: (l, 0, j)),
        out_shape=jax.ShapeDtypeStruct((depth, n_vec, n6), F32),
        compiler_params=_params(2 * d * tn * 4 + 4 * n_vec * d * LANES * 4 + (8 << 20), 2),
        name="ada",
    )(sb, w_ada, b_ada.reshape(depth, 1, n6))


def _modulate_kernel(x_ref, g_ref, sc_ref, sh_ref, o_ref):
    x = x_ref[...]
    ms = jnp.mean(x * x, axis=-1, keepdims=True)
    y = x * lax.rsqrt(ms + EPS)
    o_ref[...] = ((y * g_ref[...]) * (1.0 + sc_ref[...]) + sh_ref[...]).astype(BF16)


def _modulate(x, g, scale, shift):
    rows, d = x.shape
    tm = _pick(rows, (256, 128, 64, 32, 16))
    vec = pl.BlockSpec((1, d), lambda i: (0, 0))
    return pl.pallas_call(
        _modulate_kernel,
        grid=(rows // tm,),
        in_specs=[pl.BlockSpec((tm, d), lambda i: (i, 0)), vec, vec, vec],
        out_specs=pl.BlockSpec((tm, d), lambda i: (i, 0)),
        out_shape=jax.ShapeDtypeStruct((rows, d), BF16),
        compiler_params=_params(6 * tm * d * 4 + (8 << 20), 1),
        name="modulate",
    )(x, g.reshape(1, d), scale.reshape(1, d), shift.reshape(1, d))


def _modulate_halo_kernel(x_ref, above_ref, below_ref, g_ref, sc_ref, sh_ref, o_ref, *, tile, sub):
    t, s = pl.program_id(0), pl.program_id(1)
    n_t, n_s = pl.num_programs(0), pl.num_programs(1)
    a = g_ref[...] * (1.0 + sc_ref[...])
    b = sh_ref[...]

    def mod(x):
        ms = jnp.mean(x * x, axis=-1, keepdims=True)
        return ((x * lax.rsqrt(ms + EPS)) * a + b).astype(BF16)

    o_ref[pl.ds(pl.multiple_of(s * sub, sub), sub), :] = mod(x_ref[...])

    @pl.when(s == n_s - 1)
    def _():
        zero = jnp.zeros((HALO, x_ref.shape[1]), BF16)
        o_ref[tile:tile + HALO, :] = jnp.where(t < n_t - 1, mod(below_ref[...]), zero)
        o_ref[tile + HALO:tile + 2 * HALO, :] = jnp.where(t > 0, mod(above_ref[...]), zero)


def _modulate_halo(x, g, scale, shift, tile):
    rows, d = x.shape
    assert rows % tile == 0 and tile % HALO == 0
    sub = _pick(tile, (256, 128, 64, 32, 16))
    n_t, n_s = rows // tile, tile // sub
    hb, nblk = tile // HALO, rows // HALO
    vec = pl.BlockSpec((1, d), lambda t, s: (0, 0))
    return pl.pallas_call(
        functools.partial(_modulate_halo_kernel, tile=tile, sub=sub),
        grid=(n_t, n_s),
        in_specs=[
            pl.BlockSpec((sub, d), lambda t, s: (t * n_s + s, 0)),
            pl.BlockSpec((HALO, d), lambda t, s: (jnp.maximum(t * hb - 1, 0), 0)),
            pl.BlockSpec((HALO, d), lambda t, s: (jnp.minimum((t + 1) * hb, nblk - 1), 0)),
            vec, vec, vec,
        ],
        out_specs=pl.BlockSpec((None, tile + 2 * HALO, d), lambda t, s: (t, 0, 0)),
        out_shape=jax.ShapeDtypeStruct((n_t, tile + 2 * HALO, d), BF16),
        compiler_params=_params(4 * sub * d * 4 + 2 * (tile + 2 * HALO) * d * 2 + (8 << 20), 2),
        name="modulate_halo",
    )(x, x, x, g.reshape(1, d), scale.reshape(1, d), shift.reshape(1, d))


def _mm_kernel(*refs, n_seg, k_parts, halo, cast_w, lag, n_tiles, tm, epilogue, n_extra, n_out,
               k_dim):
    it = iter(refs)
    lhs_refs = [next(it) for _ in k_parts]
    w_refs = [next(it) for _ in range(n_seg)]
    extra_refs = [next(it) for _ in range(n_extra)]
    out_refs = [next(it) for _ in range(n_out)]
    wb_refs = [next(it) for _ in range(n_seg)] if cast_w else w_refs
    n_slots = 2 if lag else (1 if halo else 0)
    acc_slots = [[next(it) for _ in range(n_seg)] for _ in range(n_slots)]

    i = pl.program_id(1)
    if cast_w:
        ck = _pick(k_dim, (256, 128, 64, 32, 16))

        @pl.when(i == 0)
        def _():
            def cast_chunk(c, carry):
                rows = pl.ds(pl.multiple_of(c * ck, ck), ck)
                for w_ref, wb_ref in zip(w_refs, wb_refs):
                    wb_ref[rows, :] = w_ref[rows, :].astype(BF16)
                return carry
            lax.fori_loop(0, k_dim // ck, cast_chunk, 0)

    def dots(between=()):
        lhs_vals = [r[...] for r in lhs_refs]
        if between:
            kc = k_dim // len(between)
            accs = [None] * n_seg
            for c, epi_chunk in enumerate(between):
                for s, wb_ref in enumerate(wb_refs):
                    part = jnp.dot(lhs_vals[0][:, c * kc:(c + 1) * kc], wb_ref[c * kc:(c + 1) * kc, :],
                                   preferred_element_type=F32)
                    accs[s] = part if accs[s] is None else accs[s] + part
                epi_chunk()
            return accs
        accs = []
        for wb_ref in wb_refs:
            acc, k0 = None, 0
            for lhs, kp in zip(lhs_vals, k_parts):
                part = jnp.dot(lhs, wb_ref[k0:k0 + kp, :], preferred_element_type=F32)
                acc = part if acc is None else acc + part
                k0 += kp
            accs.append(acc)
        return accs

    def store(slot, between=()):
        for acc, ref in zip(dots(between), acc_slots[slot]):
            if halo:
                ref[HALO:, :] = acc[:tm + HALO, :]
                ref[:HALO, :] = acc[tm + HALO:, :]
            else:
                ref[...] = acc

    def finish(slot):
        accs = acc_slots[slot] if halo else [ref[...] for ref in acc_slots[slot]]
        epilogue(accs, extra_refs, out_refs)

    if not lag:
        if halo:
            store(0)
            finish(0)
        else:
            epilogue(dots(), extra_refs, out_refs)
        return


    @pl.when(i == 0)
    def _():
        store(0)

    for parity in range(2):
        @pl.when(jnp.logical_and(jnp.logical_and(i > 0, i < n_tiles), i % 2 == parity))
        def _():
            if halo:
                nr = min(CONV_ROWS, tm)
                store(parity, [functools.partial(epilogue, acc_slots[1 - parity], extra_refs, out_refs,
                                                 rows=(r0, nr)) for r0 in range(0, tm, nr)])
            else:
                finish(1 - parity)
                store(parity)

    @pl.when(i == n_tiles)
    def _():
        finish((n_tiles - 1) % 2)


def _mm(lhs, w, *, layer, seg_offs, n_cols, tn, tm, epilogue, extras, outs, halo=False, lag=False,
        name="mm"):
    lhs_parts = list(lhs) if isinstance(lhs, (list, tuple)) else [lhs]
    tiled = lhs_parts[0].ndim == 3
    if tiled:
        assert len(lhs_parts) == 1 and lhs_parts[0].shape[1] == tm + 2 * HALO
        rows = lhs_parts[0].shape[0] * tm
        k_parts = (lhs_parts[0].shape[2],)
    else:
        assert not halo
        rows = lhs_parts[0].shape[0]
        k_parts = tuple(p.shape[1] for p in lhs_parts)
    k_dim = sum(k_parts)
    assert rows % tm == 0 and n_cols % tn == 0
    n_tiles = rows // tm
    cast_w = w.dtype != BF16
    if lag:
        def dot_tile(i):
            return jnp.minimum(i, n_tiles - 1)

        def epi_tile(i):
            return jnp.maximum(i - 1, 0)
    else:
        def dot_tile(i):
            return i
        epi_tile = dot_tile
    if tiled:
        in_specs = [pl.BlockSpec((None, tm + 2 * HALO if halo else tm, k_dim),
                                 lambda j, i: (dot_tile(i), 0, 0))]
    else:
        in_specs = [pl.BlockSpec((tm, kp), lambda j, i: (dot_tile(i), 0)) for kp in k_parts]
    args = list(lhs_parts)
    for off in seg_offs:
        assert off % tn == 0
        ob = off // tn
        if w.ndim == 3:
            in_specs.append(pl.BlockSpec((None, k_dim, tn), lambda j, i, ob=ob: (layer, 0, ob + j)))
        else:
            in_specs.append(pl.BlockSpec((k_dim, tn), lambda j, i, ob=ob: (0, ob + j)))
        args.append(w)
    for arr, bs, im in extras:
        in_specs.append(pl.BlockSpec(bs, lambda j, i, im=im: im(j, epi_tile(i))))
        args.append(arr)
    out_specs = [pl.BlockSpec(bs, lambda j, i, im=im: im(j, epi_tile(i))) for _, bs, im in outs]
    out_shape = [sds for sds, _, _ in outs]
    n_seg = len(seg_offs)
    rows_l = tm + 2 * HALO if halo else tm
    scratch = []
    if cast_w:
        scratch += [pltpu.VMEM((k_dim, tn), BF16) for _ in range(n_seg)]
    n_slots = 2 if lag else (1 if halo else 0)
    scratch += [pltpu.VMEM((rows_l, tn), F32) for _ in range(n_slots * n_seg)]
    wbytes = w.dtype.itemsize
    est = (n_seg * k_dim * tn * (2 * wbytes + (2 if cast_w else 0))
           + 2 * rows_l * k_dim * 2 + (5 + 4 * n_seg) * rows_l * tn * 4 + (8 << 20))
    kern = functools.partial(
        _mm_kernel, n_seg=n_seg, k_parts=k_parts, halo=halo, cast_w=cast_w, lag=lag, n_tiles=n_tiles,
        tm=tm, epilogue=epilogue, n_extra=len(extras), n_out=len(outs), k_dim=k_dim)
    return pl.pallas_call(
        kern,
        grid=(n_cols // tn, n_tiles + (1 if lag else 0)),
        in_specs=in_specs,
        out_specs=out_specs,
        out_shape=out_shape,
        scratch_shapes=scratch,
        compiler_params=_params(est, 2),
        name=name,
    )(*args)


def _rms_lanes(x):
    return x * lax.rsqrt(jnp.mean(x * x, axis=-1, keepdims=True) + EPS)


def _epi_qk(accs, ex, outs, *, tn, scale):
    (acc,) = accs
    g_ref, cos_ref, sin_e_ref, sin_o_ref = ex
    (o_ref,) = outs
    g = g_ref[...]
    cos, sin_e, sin_o = cos_ref[...], sin_e_ref[...], sin_o_ref[...]
    for h in range(tn // LANES):
        hs = slice(h * LANES, (h + 1) * LANES)
        y = _rms_lanes(acc[:, hs]) * g
        r = y * cos + pltpu.roll(y, LANES - 1, 1) * sin_e + pltpu.roll(y, 1, 1) * sin_o
        o_ref[:, hs] = (r * scale if scale != 1.0 else r).astype(BF16)


def _epi_cast(accs, ex, outs):
    (acc,) = accs
    (o_ref,) = outs
    o_ref[...] = acc.astype(BF16)


def _epi_gmlp(accs, ex, outs, *, tm, tn):
    u, gv = accs
    gn_ref, ws_ref, bs_ref = ex
    (o_ref,) = outs
    gu = jax.nn.gelu(u)
    ggv = jax.nn.gelu(gv)
    nc = tm // LANES
    for g in range(tn // LANES):
        gs = slice(g * LANES, (g + 1) * LANES)
        vn = (_rms_lanes(ggv[:, gs]) * gn_ref[:, gs]).astype(BF16)
        ws = ws_ref[g].astype(BF16)
        bias = bs_ref[g]
        for c in range(nc):
            cs = slice(c * LANES, (c + 1) * LANES)
            res = jnp.dot(ws, vn[cs, :], preferred_element_type=F32)
            o_ref[cs, gs] = (gu[cs, gs] * (res + bias)).astype(BF16)


CONV_ROWS = 64


def _row_windows(ref, r0, nr):
    base = HALO + r0
    return ref[base - 1:base - 1 + nr, :], ref[base:base + nr, :], ref[base + 1:base + 1 + nr, :]


def _conv_chunks(tm, rows):
    nr = min(CONV_ROWS, tm)
    return [rows] if rows is not None else [(r0, nr) for r0 in range(0, tm, nr)]


def _epi_ffn_up(acc_refs, ex, outs, *, tm, rows=None):
    g_ref, u_ref = acc_refs
    cwg_ref, cwu_ref, cbg_ref, cbu_ref = ex
    (o_ref,) = outs
    for r0, nr in _conv_chunks(tm, rows):
        gp, gm, gn = _row_windows(g_ref, r0, nr)
        up, um, un = _row_windows(u_ref, r0, nr)
        ag = gp * cwg_ref[0:1, :] + gm * cwg_ref[1:2, :] + gn * cwg_ref[2:3, :] + cbg_ref[...]
        au = up * cwu_ref[0:1, :] + um * cwu_ref[1:2, :] + un * cwu_ref[2:3, :] + cbu_ref[...]
        o_ref[r0:r0 + nr, :] = (jax.nn.silu(ag) * au).astype(BF16)


def _epi_odd_conv(acc_refs, ex, outs, *, tm, rows=None):
    bg_ref, cg_ref, hx_ref = acc_refs
    (cw_ref,) = ex
    (o_ref,) = outs
    for r0, nr in _conv_chunks(tm, rows):
        cp, cm, cn = _row_windows(cg_ref, r0, nr)
        hp, hm, hn = _row_windows(hx_ref, r0, nr)
        conv = (cp * hp) * cw_ref[0:1, :] + (cm * hm) * cw_ref[1:2, :] + (cn * hn) * cw_ref[2:3, :]
        o_ref[r0:r0 + nr, :] = (bg_ref[HALO + r0:HALO + r0 + nr, :] * conv).astype(BF16)


def _epi_residual(accs, ex, outs):
    (acc,) = accs
    x_ref, gate_ref = ex
    (o_ref,) = outs
    o_ref[...] = x_ref[...] + gate_ref[...] * acc


def _epi_chan_dft(accs, ex, outs, *, tn):
    (acc,) = accs
    (cs_ref,) = ex
    (o_ref,) = outs
    cs = cs_ref[...]
    for g in range(tn // LANES):
        gs = slice(g * LANES, (g + 1) * LANES)
        ab = jnp.dot(acc[:, gs].astype(BF16), cs, preferred_element_type=F32)
        o_ref[0, :, gs] = ab[:, :LANES].astype(BF16)
        o_ref[1, :, gs] = ab[:, LANES:].astype(BF16)


def _epi_scale(accs, ex, outs, *, scale):
    (acc,) = accs
    (o_ref,) = outs
    o_ref[...] = (acc * scale).astype(BF16)


_NT = (((1,), (1,)), ((), ()))
_TN = (((0,), (0,)), ((), ()))


def _attn_kernel(*refs, tq, tk, n_lat, group):
    if n_lat:
        q_ref, kc_ref, vc_ref, k_ref, v_ref, o_ref = refs
    else:
        q_ref, kc_ref, vc_ref, o_ref = refs
    r = group * tq
    q = jnp.concatenate([q_ref[:, g * LANES:(g + 1) * LANES] for g in range(group)], axis=0)
    tiles = [(kc_ref, vc_ref, slice(None))]
    tiles += [(k_ref, v_ref, slice(t * tk, (t + 1) * tk)) for t in range(n_lat)]

    def scores(t):
        kr, _, rows = tiles[t]
        return lax.dot_general(kr[rows, :], q, _NT, preferred_element_type=F32)

    m = jnp.full((1, r), NEG_BIG, F32)
    l = jnp.zeros((1, r), F32)
    acc = jnp.zeros((LANES, r), F32)
    s_next = scores(0)
    for t in range(len(tiles)):
        s = s_next
        if t + 1 < len(tiles):
            s_next = scores(t + 1)
        _, vr, rows = tiles[t]
        m_new = jnp.maximum(m, jnp.max(s, axis=0, keepdims=True))
        alpha = jnp.exp2(m - m_new)
        p = jnp.exp2(s - m_new)
        l = alpha * l + jnp.sum(p, axis=0, keepdims=True)
        pv = lax.dot_general(vr[rows, :], p.astype(BF16), _TN, preferred_element_type=F32)
        acc = alpha * acc + pv
        m = m_new
    out_t = (acc * (1.0 / l)).T
    for g in range(group):
        o_ref[:, g * LANES:(g + 1) * LANES] = out_t[g * tq:(g + 1) * tq, :].astype(BF16)


def _attention(q, kc, vc, k_lat, v_lat, *, n_kv, group, tk):
    nq = q.shape[0]
    lc = kc.shape[0]
    tq = _pick(nq, (256, 128))
    gw = group * LANES
    ctx_spec = pl.BlockSpec((lc, LANES), lambda h, i: (0, h))
    in_specs = [pl.BlockSpec((tq, gw), lambda h, i: (i, h)), ctx_spec, ctx_spec]
    args = [q, kc, vc]
    n_lat = 0
    if k_lat is not None:
        n_lat = k_lat.shape[0] // tk
        lat_spec = pl.BlockSpec((k_lat.shape[0], LANES), lambda h, i: (0, h))
        in_specs += [lat_spec, lat_spec]
        args += [k_lat, v_lat]
    r = group * tq
    est = (4 * k_lat.shape[0] * LANES * 2 * 2 if n_lat else 0) + 6 * max(tk, lc) * r * 4 + (8 << 20)
    return pl.pallas_call(
        functools.partial(_attn_kernel, tq=tq, tk=tk, n_lat=n_lat, group=group),
        grid=(n_kv, nq // tq),
        in_specs=in_specs,
        out_specs=pl.BlockSpec((tq, gw), lambda h, i: (i, h)),
        out_shape=jax.ShapeDtypeStruct((nq, n_kv * gw), BF16),
        compiler_params=_params(est, 2),
        name="attention",
    )(*args)


def _rope_tables(n, identity):
    if identity:
        return jnp.ones((n, LANES), F32), jnp.zeros((n, LANES), F32), jnp.zeros((n, LANES), F32)
    rows = n // GRID_W
    row = jnp.repeat(jnp.arange(rows, dtype=F32), GRID_W)
    col = jnp.tile(jnp.arange(GRID_W, dtype=F32), rows)
    half = LANES // 2
    inv_freq = ROPE_THETA ** (-jnp.arange(0, half, 2, dtype=F32) / half)
    ang = jnp.concatenate([row[:, None] * inv_freq, col[:, None] * inv_freq], axis=-1)
    cos, sin = jnp.cos(ang), jnp.sin(ang)
    cos_i = jnp.repeat(cos, 2, axis=-1)
    sin_i = jnp.repeat(sin, 2, axis=-1)
    even = (jnp.arange(LANES) % 2 == 0)[None, :]
    return cos_i, jnp.where(even, -sin_i, 0.0), jnp.where(even, 0.0, sin_i)


def _dft_tables(n):
    idx = jnp.arange(n, dtype=jnp.int32)
    jk = (idx[:, None] * idx[None, :]) % n
    ang = jk.astype(F32) * (2.0 * math.pi / n)
    return jnp.cos(ang), jnp.sin(ang)


def _dft_lhs(n):
    n0 = LANES if n % LANES == 0 and n > LANES else 1
    n1 = n // n0
    kk = jnp.arange(2 * n, dtype=jnp.int32)
    k = kk % n
    phase = jnp.where(kk >= n, 0.5 * math.pi, 0.0).astype(F32)
    j1 = jnp.arange(n1, dtype=jnp.int32)[:, None]
    j0 = jnp.arange(n0, dtype=jnp.int32)[:, None]
    coarse = ((j1 * k[None, :]) % n1).astype(F32) * (2.0 * math.pi / n1)
    fine = ((j0 * k[None, :]) % n).astype(F32) * (2.0 * math.pi / n) + phase
    ca, sa = jnp.cos(coarse)[:, None, :], jnp.sin(coarse)[:, None, :]
    cb, sb = jnp.cos(fine)[None, :, :], jnp.sin(fine)[None, :, :]
    return (ca * cb - sa * sb).astype(BF16).reshape(n, 2 * n)


def _row_tile(rows):
    return _pick(rows, (1024, 512, 256, 128))


def _residual(lhs, w, layer, x, gate, name):
    parts = list(lhs) if isinstance(lhs, (list, tuple)) else [lhs]
    rows, k_dim = parts[0].shape[0], sum(p.shape[1] for p in parts)
    d = x.shape[1]
    if k_dim > 4096:
        tn, tm = _pick(d, (512, 256, 128)), _pick(rows, (512, 256, 128))
    else:
        tn, tm = _pick(d, (1024, 512, 256, 128)), _pick(rows, (512, 256, 128))
    (out,) = _mm(
        lhs, w, layer=layer, seg_offs=[0], n_cols=d, tn=tn, tm=tm, epilogue=_epi_residual,
        extras=[(x, (tm, tn), lambda j, i: (i, j)), (gate.reshape(1, d), (1, tn), lambda j, i: (0, j))],
        outs=[(jax.ShapeDtypeStruct((rows, d), F32), (tm, tn), lambda j, i: (i, j))],
        name=name)
    return out


def _ffn(x, mod, w_up, conv_w, conv_b, w_down, layer, gate):
    rows = x.shape[0]
    d_ff = w_down.shape[1]
    tn = _pick(d_ff, (256, 128))
    tm = _row_tile(rows)
    nb = d_ff // tn
    h2 = _modulate_halo(x, *mod, tile=tm)
    cw = conv_w[layer]
    cb = conv_b[layer].reshape(1, 2 * d_ff)
    (act,) = _mm(
        h2, w_up, layer=layer, seg_offs=[0, d_ff], n_cols=d_ff, tn=tn, tm=tm, halo=True,
        lag=rows > tm, epilogue=functools.partial(_epi_ffn_up, tm=tm),
        extras=[(cw, (3, tn), lambda j, i: (0, j)), (cw, (3, tn), lambda j, i: (0, nb + j)),
                (cb, (1, tn), lambda j, i: (0, j)), (cb, (1, tn), lambda j, i: (0, nb + j))],
        outs=[(jax.ShapeDtypeStruct((rows, d_ff), BF16), (tm, tn), lambda j, i: (i, j))],
        name="ffn_up")
    return _residual(act, w_down, layer, x, gate, "ffn_down")


def _even_in(h, w_in, e, q_g, k_g, gn_g, w_s, bs_b, tabs, dims, *, want_q, want_gmlp):
    a_q_w, a_kv_w, b_w = dims
    rows = h.shape[0]
    tm = _row_tile(rows)
    tn = _pick(a_kv_w, (512, 256, 128))
    cos, sin_e, sin_o = tabs
    tab_specs = [(t, (tm, LANES), lambda j, i: (i, 0)) for t in (cos, sin_e, sin_o)]
    q = gm = None

    def proj(off, n_cols, epilogue, extras, name, lag=False):
        (out,) = _mm(
            h, w_in, layer=e, seg_offs=[off], n_cols=n_cols, tn=tn, tm=tm, epilogue=epilogue,
            extras=extras, lag=lag and rows > tm,
            outs=[(jax.ShapeDtypeStruct((rows, n_cols), BF16), (tm, tn), lambda j, i: (i, j))],
            name=name)
        return out

    if want_q:
        qscale = float(LANES ** -0.5 * math.log2(math.e))
        q = proj(0, a_q_w, functools.partial(_epi_qk, tn=tn, scale=qscale),
                 [(q_g.reshape(1, LANES), (1, LANES), lambda j, i: (0, 0))] + tab_specs, "q_proj",
                 lag=True)
    k = proj(a_q_w, a_kv_w, functools.partial(_epi_qk, tn=tn, scale=1.0),
             [(k_g.reshape(1, LANES), (1, LANES), lambda j, i: (0, 0))] + tab_specs, "k_proj",
             lag=True)
    v = proj(a_q_w + a_kv_w, a_kv_w, _epi_cast, [], "v_proj")
    if want_gmlp:
        tg = _pick(b_w, (256, 128))
        ng = tg // LANES
        (gm,) = _mm(
            h, w_in, layer=e, seg_offs=[a_q_w + 2 * a_kv_w, a_q_w + 2 * a_kv_w + b_w], n_cols=b_w,
            tn=tg, tm=tm, epilogue=functools.partial(_epi_gmlp, tm=tm, tn=tg),
            extras=[(gn_g.reshape(1, b_w), (1, tg), lambda j, i: (0, j)),
                    (w_s, (ng, LANES, LANES), lambda j, i: (j, 0, 0)),
                    (bs_b, (ng, LANES, LANES), lambda j, i: (j, 0, 0))],
            outs=[(jax.ShapeDtypeStruct((rows, b_w), BF16), (tm, tg), lambda j, i: (i, j))],
            name="gmlp")
    return q, k, v, gm


def _fourier(f_ab, n_rows, d_w, dft_lhs):
    tm = _pick(n_rows, (256, 128))
    tn = _pick(d_w, (256, 128))
    scale = float(1.0 / math.sqrt(n_rows * LANES))
    (out,) = _mm(
        dft_lhs, f_ab.reshape(2 * n_rows, d_w), layer=0, seg_offs=[0], n_cols=d_w, tn=tn, tm=tm,
        epilogue=functools.partial(_epi_scale, scale=scale), extras=[],
        outs=[(jax.ShapeDtypeStruct((n_rows, d_w), BF16), (tm, tn), lambda j, i: (i, j))],
        name="pos_dft")
    return out


def _odd_mixer(x, mod, w_in, o, conv_w, c_w, d_w, chan_cs, dft_lhs):
    rows = x.shape[0]
    tn = _pick(c_w, (256, 128))
    tm = _pick(rows, (512, 256, 128))
    h = _modulate_halo(x, *mod, tile=tm)
    (y_c,) = _mm(
        h, w_in, layer=o, seg_offs=[0, c_w, 2 * c_w], n_cols=c_w, tn=tn, tm=tm, halo=True,
        epilogue=functools.partial(_epi_odd_conv, tm=tm),
        extras=[(conv_w[o], (3, tn), lambda j, i: (0, j))],
        outs=[(jax.ShapeDtypeStruct((rows, c_w), BF16), (tm, tn), lambda j, i: (i, j))],
        name="odd_conv")
    tf = _pick(d_w, (512, 256, 128))
    tmf = tm
    (f_ab,) = _mm(
        h, w_in, layer=o, seg_offs=[3 * c_w], n_cols=d_w, tn=tf, tm=tmf,
        lag=rows > tmf, epilogue=functools.partial(_epi_chan_dft, tn=tf),
        extras=[(chan_cs, (LANES, 2 * LANES), lambda j, i: (0, 0))],
        outs=[(jax.ShapeDtypeStruct((2, rows, d_w), BF16), (2, tmf, tf), lambda j, i: (0, i, j))],
        name="chan_dft")
    four = _fourier(f_ab, rows, d_w, dft_lhs)
    return [y_c, four]


def kernel(x, c, ctx, c_ctx, w_ada, b_ada, norm1_g, norm2_g, w_ffn_up, ffn_conv_w, ffn_conv_b,
           w_ffn_down, w_in_a, w_out_a, q_norm_g, k_norm_g, gmlp_norm_g, gmlp_w_s, gmlp_b_s,
           w_in_c, w_out_c, conv_w_c):
    assert x.shape[0] == 1 and c.shape[0] == 1 and ctx.shape[0] == 1
    n, d = x.shape[1], x.shape[2]
    lc = ctx.shape[1]
    depth = w_ada.shape[0]
    assert q_norm_g.shape[-1] == LANES and gmlp_w_s.shape[2] == LANES
    b_w = gmlp_norm_g.shape[-1]
    a_q_w = w_out_a.shape[1] - b_w
    a_kv_w = (w_in_a.shape[-1] - a_q_w - 2 * b_w) // 2
    n_kv = a_kv_w // LANES
    group = a_q_w // a_kv_w
    c_w = conv_w_c.shape[-1]
    d_w = w_in_c.shape[-1] - 3 * c_w
    dims = (a_q_w, a_kv_w, b_w)

    x = x[0]
    xc = ctx[0]
    s_vecs = jnp.stack([jax.nn.silu(c[0]), jax.nn.silu(c_ctx)])
    ada = _ada(s_vecs, w_ada, b_ada)

    def mod_vecs(layer, stream):
        return [ada[layer, stream, p * d:(p + 1) * d] for p in range(6)]

    tk = _pick(n, (1024, 512, 256, 128))
    tabs_lat = _rope_tables(n, identity=False)
    tabs_ctx = _rope_tables(lc, identity=True)
    bs_b = jnp.broadcast_to(gmlp_b_s[..., None], gmlp_b_s.shape + (LANES,))

    any_odd = depth > 1
    if any_odd:
        cc, sc = _dft_tables(LANES)
        chan_cs = jnp.concatenate([cc, sc], axis=1).astype(BF16)
        dft_lat = _dft_lhs(n)
        dft_ctx = _dft_lhs(lc)

    for i in range(depth):
        ctx_later = any(j % 2 == 0 for j in range(i + 1, depth))
        sh1, sc1, g1, sh2, sc2, g2 = mod_vecs(i, 0)
        csh1, csc1, cg1, csh2, csc2, cg2 = mod_vecs(i, 1)
        if i % 2 == 0:
            e = i // 2
            h = _modulate(x, norm1_g[i], sc1, sh1)
            hc = _modulate(xc, norm1_g[i], csc1, csh1)
            q, k, v, gm = _even_in(h, w_in_a, e, q_norm_g[e], k_norm_g[e], gmlp_norm_g[e],
                                   gmlp_w_s[e], bs_b[e], tabs_lat, dims,
                                   want_q=True, want_gmlp=True)
            qc, kc, vc, gmc = _even_in(hc, w_in_a, e, q_norm_g[e], k_norm_g[e], gmlp_norm_g[e],
                                       gmlp_w_s[e], bs_b[e], tabs_ctx, dims,
                                       want_q=ctx_later, want_gmlp=ctx_later)
            attn = _attention(q, kc, vc, k, v, n_kv=n_kv, group=group, tk=tk)
            mix = [attn, gm]
            x = _residual(mix, w_out_a, e, x, g1, "mix_out")
            if ctx_later:
                attn_c = _attention(qc, kc, vc, None, None, n_kv=n_kv, group=group, tk=tk)
                mix_c = [attn_c, gmc]
                xc = _residual(mix_c, w_out_a, e, xc, cg1, "mix_out")
        else:
            o = i // 2
            mix = _odd_mixer(x, (norm1_g[i], sc1, sh1), w_in_c, o, conv_w_c, c_w, d_w, chan_cs, dft_lat)
            x = _residual(mix, w_out_c, o, x, g1, "mix_out")
            if ctx_later:
                mix_c = _odd_mixer(xc, (norm1_g[i], csc1, csh1), w_in_c, o, conv_w_c, c_w, d_w,
                                   chan_cs, dft_ctx)
                xc = _residual(mix_c, w_out_c, o, xc, cg1, "mix_out")
        x = _ffn(x, (norm2_g[i], sc2, sh2), w_ffn_up, ffn_conv_w, ffn_conv_b, w_ffn_down, i, g2)
        if ctx_later:
            xc = _ffn(xc, (norm2_g[i], csc2, csh2), w_ffn_up, ffn_conv_w, ffn_conv_b, w_ffn_down, i, cg2)
    return x[None]
```

```python
import functools
import math

import jax
import jax.numpy as jnp
from jax import lax
from jax.experimental import pallas as pl
from jax.experimental.pallas import tpu as pltpu

BF16 = jnp.bfloat16
F32 = jnp.float32

GRID_W = 64
ROPE_THETA = 10000.0
EPS = 1e-6
LANES = 128
HALO = 16
MXU_DEPTH = 256
VMEM_CAP = 60 * 1024 * 1024
NEG_BIG = -1e30


def _pick(n, prefs):
    for p in prefs:
        if n % p == 0:
            return p
    return n


def _params(vmem_bytes, n_axes):
    return pltpu.CompilerParams(
        dimension_semantics=("arbitrary",) * n_axes,
        vmem_limit_bytes=int(min(VMEM_CAP, vmem_bytes)),
    )


def _ada_kernel(sb_ref, w_ref, b_ref, o_ref, *, tn, n_vec):
    for t in range(tn // LANES):
        sl = slice(t * LANES, (t + 1) * LANES)
        w = w_ref[:, sl]
        for r in range(n_vec):
            o_ref[r:r + 1, sl] = jnp.sum(w * sb_ref[r], axis=0, keepdims=True) + b_ref[:, sl]


def _ada(s_vecs, w_ada, b_ada):
    depth, d, n6 = w_ada.shape
    n_vec = s_vecs.shape[0]
    tn = _pick(n6, (512, 256, 128))
    sb = jnp.broadcast_to(s_vecs[:, :, None], (n_vec, d, LANES))
    return pl.pallas_call(
        functools.partial(_ada_kernel, tn=tn, n_vec=n_vec),
        grid=(depth, n6 // tn),
        in_specs=[
            pl.BlockSpec((n_vec, d, LANES), lambda l, j: (0, 0, 0)),
            pl.BlockSpec((None, d, tn), lambda l, j: (l, 0, j)),
            pl.BlockSpec((None, 1, tn), lambda l, j: (l, 0, j)),
        ],
        out_specs=pl.BlockSpec((None, n_vec, tn), lambda l, j: (l, 0, j)),
        out_shape=jax.ShapeDtypeStruct((depth, n_vec, n6), F32),
        compiler_params=_params(2 * d * tn * 4 + 4 * n_vec * d * LANES * 4 + (8 << 20), 2),
        name="ada",
    )(sb, w_ada, b_ada.reshape(depth, 1, n6))


def _modulate_kernel(x_ref, g_ref, sc_ref, sh_ref, o_ref):
    x = x_ref[...]
    ms = jnp.mean(x * x, axis=-1, keepdims=True)
    y = x * lax.rsqrt(ms + EPS)
    o_ref[...] = ((y * g_ref[...]) * (1.0 + sc_ref[...]) + sh_ref[...]).astype(BF16)


def _modulate(x, g, scale, shift):
    rows, d = x.shape
    tm = _pick(rows, (256, 128, 64, 32, 16))
    vec = pl.BlockSpec((1, d), lambda i: (0, 0))
    return pl.pallas_call(
        _modulate_kernel,
        grid=(rows // tm,),
        in_specs=[pl.BlockSpec((tm, d), lambda i: (i, 0)), vec, vec, vec],
        out_specs=pl.BlockSpec((tm, d), lambda i: (i, 0)),
        out_shape=jax.ShapeDtypeStruct((rows, d), BF16),
        compiler_params=_params(6 * tm * d * 4 + (8 << 20), 1),
        name="modulate",
    )(x, g.reshape(1, d), scale.reshape(1, d), shift.reshape(1, d))


def _modulate_halo_kernel(x_ref, above_ref, below_ref, g_ref, sc_ref, sh_ref, o_ref, *, tile, sub):
    t, s = pl.program_id(0), pl.program_id(1)
    n_t, n_s = pl.num_programs(0), pl.num_programs(1)
    a = g_ref[...] * (1.0 + sc_ref[...])
    b = sh_ref[...]

    def mod(x):
        ms = jnp.mean(x * x, axis=-1, keepdims=True)
        return ((x * lax.rsqrt(ms + EPS)) * a + b).astype(BF16)

    o_ref[pl.ds(pl.multiple_of(s * sub, sub), sub), :] = mod(x_ref[...])

    @pl.when(s == n_s - 1)
    def _():
        zero = jnp.zeros((HALO, x_ref.shape[1]), BF16)
        o_ref[tile:tile + HALO, :] = jnp.where(t < n_t - 1, mod(below_ref[...]), zero)
        o_ref[tile + HALO:tile + 2 * HALO, :] = jnp.where(t > 0, mod(above_ref[...]), zero)


def _modulate_halo(x, g, scale, shift, tile):
    rows, d = x.shape
    assert rows % tile == 0 and tile % HALO == 0
    sub = _pick(tile, (256, 128, 64, 32, 16))
    n_t, n_s = rows // tile, tile // sub
    hb, nblk = tile // HALO, rows // HALO
    vec = pl.BlockSpec((1, d), lambda t, s: (0, 0))
    return pl.pallas_call(
        functools.partial(_modulate_halo_kernel, tile=tile, sub=sub),
        grid=(n_t, n_s),
        in_specs=[
            pl.BlockSpec((sub, d), lambda t, s: (t * n_s + s, 0)),
            pl.BlockSpec((HALO, d), lambda t, s: (jnp.maximum(t * hb - 1, 0), 0)),
            pl.BlockSpec((HALO, d), lambda t, s: (jnp.minimum((t + 1) * hb, nblk - 1), 0)),
            vec, vec, vec,
        ],
        out_specs=pl.BlockSpec((None, tile + 2 * HALO, d), lambda t, s: (t, 0, 0)),
        out_shape=jax.ShapeDtypeStruct((n_t, tile + 2 * HALO, d), BF16),
        compiler_params=_params(4 * sub * d * 4 + 2 * (tile + 2 * HALO) * d * 2 + (8 << 20), 2),
        name="modulate_halo",
    )(x, x, x, g.reshape(1, d), scale.reshape(1, d), shift.reshape(1, d))


def _mm_kernel(*refs, n_seg, k_parts, halo, cast_w, lag, n_tiles, tm, epilogue, n_extra, n_out,
               k_dim):
    it = iter(refs)
    lhs_refs = [next(it) for _ in k_parts]
    w_refs = [next(it) for _ in range(n_seg)]
    extra_refs = [next(it) for _ in range(n_extra)]
    out_refs = [next(it) for _ in range(n_out)]
    wb_refs = [next(it) for _ in range(n_seg)] if cast_w else w_refs
    n_slots = 2 if lag else (1 if halo else 0)
    acc_slots = [[next(it) for _ in range(n_seg)] for _ in range(n_slots)]

    i = pl.program_id(1)
    if cast_w:
        ck = _pick(k_dim, (256, 128, 64, 32, 16))

        @pl.when(i == 0)
        def _():
            def cast_chunk(c, carry):
                rows = pl.ds(pl.multiple_of(c * ck, ck), ck)
                for w_ref, wb_ref in zip(w_refs, wb_refs):
                    wb_ref[rows, :] = w_ref[rows, :].astype(BF16)
                return carry
            lax.fori_loop(0, k_dim // ck, cast_chunk, 0)

    def dots(between=()):
        lhs_vals = [r[...] for r in lhs_refs]
        if between:
            kc = k_dim // len(between)
            accs = [None] * n_seg
            for c, epi_chunk in enumerate(between):
                for s, wb_ref in enumerate(wb_refs):
                    part = jnp.dot(lhs_vals[0][:, c * kc:(c + 1) * kc], wb_ref[c * kc:(c + 1) * kc, :],
                                   preferred_element_type=F32)
                    accs[s] = part if accs[s] is None else accs[s] + part
                epi_chunk()
            return accs
        accs = []
        for wb_ref in wb_refs:
            acc, k0 = None, 0
            for lhs, kp in zip(lhs_vals, k_parts):
                part = jnp.dot(lhs, wb_ref[k0:k0 + kp, :], preferred_element_type=F32)
                acc = part if acc is None else acc + part
                k0 += kp
            accs.append(acc)
        return accs

    def store(slot, between=()):
        for acc, ref in zip(dots(between), acc_slots[slot]):
            if halo:
                ref[HALO:, :] = acc[:tm + HALO, :]
                ref[:HALO, :] = acc[tm + HALO:, :]
            else:
                ref[...] = acc

    def finish(slot):
        accs = acc_slots[slot] if halo else [ref[...] for ref in acc_slots[slot]]
        epilogue(accs, extra_refs, out_refs)

    if not lag:
        if halo:
            store(0)
            finish(0)
        else:
            epilogue(dots(), extra_refs, out_refs)
        return


    @pl.when(i == 0)
    def _():
        store(0)

    for parity in range(2):
        @pl.when(jnp.logical_and(jnp.logical_and(i > 0, i < n_tiles), i % 2 == parity))
        def _():
            if halo:
                nr = max(8, tm // max(1, k_dim // MXU_DEPTH))
                store(parity, [functools.partial(epilogue, acc_slots[1 - parity], extra_refs, out_refs,
                                                 rows=(r0, nr)) for r0 in range(0, tm, nr)])
            else:
                finish(1 - parity)
                store(parity)

    @pl.when(i == n_tiles)
    def _():
        finish((n_tiles - 1) % 2)


def _mm(lhs, w, *, layer, seg_offs, n_cols, tn, tm, epilogue, extras, outs, halo=False, lag=False,
        name="mm"):
    lhs_parts = list(lhs) if isinstance(lhs, (list, tuple)) else [lhs]
    tiled = lhs_parts[0].ndim == 3
    if tiled:
        assert len(lhs_parts) == 1 and lhs_parts[0].shape[1] == tm + 2 * HALO
        rows = lhs_parts[0].shape[0] * tm
        k_parts = (lhs_parts[0].shape[2],)
    else:
        assert not halo
        rows = lhs_parts[0].shape[0]
        k_parts = tuple(p.shape[1] for p in lhs_parts)
    k_dim = sum(k_parts)
    assert rows % tm == 0 and n_cols % tn == 0
    n_tiles = rows // tm
    cast_w = w.dtype != BF16
    if lag:
        def dot_tile(i):
            return jnp.minimum(i, n_tiles - 1)

        def epi_tile(i):
            return jnp.maximum(i - 1, 0)
    else:
        def dot_tile(i):
            return i
        epi_tile = dot_tile
    if tiled:
        in_specs = [pl.BlockSpec((None, tm + 2 * HALO if halo else tm, k_dim),
                                 lambda j, i: (dot_tile(i), 0, 0))]
    else:
        in_specs = [pl.BlockSpec((tm, kp), lambda j, i: (dot_tile(i), 0)) for kp in k_parts]
    args = list(lhs_parts)
    for off in seg_offs:
        assert off % tn == 0
        ob = off // tn
        if w.ndim == 3:
            in_specs.append(pl.BlockSpec((None, k_dim, tn), lambda j, i, ob=ob: (layer, 0, ob + j)))
        else:
            in_specs.append(pl.BlockSpec((k_dim, tn), lambda j, i, ob=ob: (0, ob + j)))
        args.append(w)
    for arr, bs, im in extras:
        in_specs.append(pl.BlockSpec(bs, lambda j, i, im=im: im(j, epi_tile(i))))
        args.append(arr)
    out_specs = [pl.BlockSpec(bs, lambda j, i, im=im: im(j, epi_tile(i))) for _, bs, im in outs]
    out_shape = [sds for sds, _, _ in outs]
    n_seg = len(seg_offs)
    rows_l = tm + 2 * HALO if halo else tm
    scratch = []
    if cast_w:
        scratch += [pltpu.VMEM((k_dim, tn), BF16) for _ in range(n_seg)]
    n_slots = 2 if lag else (1 if halo else 0)
    scratch += [pltpu.VMEM((rows_l, tn), F32) for _ in range(n_slots * n_seg)]
    wbytes = w.dtype.itemsize
    est = (n_seg * k_dim * tn * (2 * wbytes + (2 if cast_w else 0))
           + 2 * rows_l * k_dim * 2 + (5 + 4 * n_seg) * rows_l * tn * 4 + (8 << 20))
    kern = functools.partial(
        _mm_kernel, n_seg=n_seg, k_parts=k_parts, halo=halo, cast_w=cast_w, lag=lag, n_tiles=n_tiles,
        tm=tm, epilogue=epilogue, n_extra=len(extras), n_out=len(outs), k_dim=k_dim)
    return pl.pallas_call(
        kern,
        grid=(n_cols // tn, n_tiles + (1 if lag else 0)),
        in_specs=in_specs,
        out_specs=out_specs,
        out_shape=out_shape,
        scratch_shapes=scratch,
        compiler_params=_params(est, 2),
        name=name,
    )(*args)


def _rms_lanes(x):
    return x * lax.rsqrt(jnp.mean(x * x, axis=-1, keepdims=True) + EPS)


def _epi_qk(accs, ex, outs, *, tn, scale):
    (acc,) = accs
    g_ref, cos_ref, sin_e_ref, sin_o_ref = ex
    (o_ref,) = outs
    g = g_ref[...]
    cos, sin_e, sin_o = cos_ref[...], sin_e_ref[...], sin_o_ref[...]
    for h in range(tn // LANES):
        hs = slice(h * LANES, (h + 1) * LANES)
        y = _rms_lanes(acc[:, hs]) * g
        r = y * cos + pltpu.roll(y, LANES - 1, 1) * sin_e + pltpu.roll(y, 1, 1) * sin_o
        o_ref[:, hs] = (r * scale if scale != 1.0 else r).astype(BF16)


def _epi_cast(accs, ex, outs):
    (acc,) = accs
    (o_ref,) = outs
    o_ref[...] = acc.astype(BF16)


def _epi_gmlp(accs, ex, outs, *, tm, tn):
    u, gv = accs
    gn_ref, ws_ref, bs_ref = ex
    (o_ref,) = outs
    gu = jax.nn.gelu(u)
    ggv = jax.nn.gelu(gv)
    nc = tm // LANES
    for g in range(tn // LANES):
        gs = slice(g * LANES, (g + 1) * LANES)
        vn = (_rms_lanes(ggv[:, gs]) * gn_ref[:, gs]).astype(BF16)
        ws = ws_ref[g].astype(BF16)
        bias = bs_ref[g]
        for c in range(nc):
            cs = slice(c * LANES, (c + 1) * LANES)
            res = jnp.dot(ws, vn[cs, :], preferred_element_type=F32)
            o_ref[cs, gs] = (gu[cs, gs] * (res + bias)).astype(BF16)


CONV_ROWS = 64


def _row_windows(ref, r0, nr):
    base = HALO + r0
    return ref[base - 1:base - 1 + nr, :], ref[base:base + nr, :], ref[base + 1:base + 1 + nr, :]


def _conv_chunks(tm, rows):
    nr = min(CONV_ROWS, tm)
    return [rows] if rows is not None else [(r0, nr) for r0 in range(0, tm, nr)]


def _epi_ffn_up(acc_refs, ex, outs, *, tm, rows=None):
    g_ref, u_ref = acc_refs
    cwg_ref, cwu_ref, cbg_ref, cbu_ref = ex
    (o_ref,) = outs
    for r0, nr in _conv_chunks(tm, rows):
        gp, gm, gn = _row_windows(g_ref, r0, nr)
        up, um, un = _row_windows(u_ref, r0, nr)
        ag = gp * cwg_ref[0:1, :] + gm * cwg_ref[1:2, :] + gn * cwg_ref[2:3, :] + cbg_ref[...]
        au = up * cwu_ref[0:1, :] + um * cwu_ref[1:2, :] + un * cwu_ref[2:3, :] + cbu_ref[...]
        o_ref[r0:r0 + nr, :] = (jax.nn.silu(ag) * au).astype(BF16)


def _epi_odd_conv(acc_refs, ex, outs, *, tm, rows=None):
    bg_ref, cg_ref, hx_ref = acc_refs
    (cw_ref,) = ex
    (o_ref,) = outs
    for r0, nr in _conv_chunks(tm, rows):
        cp, cm, cn = _row_windows(cg_ref, r0, nr)
        hp, hm, hn = _row_windows(hx_ref, r0, nr)
        conv = (cp * hp) * cw_ref[0:1, :] + (cm * hm) * cw_ref[1:2, :] + (cn * hn) * cw_ref[2:3, :]
        o_ref[r0:r0 + nr, :] = (bg_ref[HALO + r0:HALO + r0 + nr, :] * conv).astype(BF16)


def _epi_residual(accs, ex, outs):
    (acc,) = accs
    x_ref, gate_ref = ex
    (o_ref,) = outs
    o_ref[...] = x_ref[...] + gate_ref[...] * acc


def _epi_chan_dft(accs, ex, outs, *, tn):
    (acc,) = accs
    (cs_ref,) = ex
    (o_ref,) = outs
    cs = cs_ref[...]
    for g in range(tn // LANES):
        gs = slice(g * LANES, (g + 1) * LANES)
        ab = jnp.dot(acc[:, gs].astype(BF16), cs, preferred_element_type=F32)
        o_ref[0, :, gs] = ab[:, :LANES].astype(BF16)
        o_ref[1, :, gs] = ab[:, LANES:].astype(BF16)


def _epi_scale(accs, ex, outs, *, scale):
    (acc,) = accs
    (o_ref,) = outs
    o_ref[...] = (acc * scale).astype(BF16)


_NT = (((1,), (1,)), ((), ()))
_TN = (((0,), (0,)), ((), ()))


def _attn_kernel(*refs, tq, tk, n_lat, group):
    if n_lat:
        q_ref, kc_ref, vc_ref, k_ref, v_ref, o_ref = refs
    else:
        q_ref, kc_ref, vc_ref, o_ref = refs
    r = group * tq
    q = jnp.concatenate([q_ref[:, g * LANES:(g + 1) * LANES] for g in range(group)], axis=0)
    tiles = [(kc_ref, vc_ref, slice(None))]
    tiles += [(k_ref, v_ref, slice(t * tk, (t + 1) * tk)) for t in range(n_lat)]

    def scores(t):
        kr, _, rows = tiles[t]
        return lax.dot_general(kr[rows, :], q, _NT, preferred_element_type=F32)

    m = jnp.full((1, r), NEG_BIG, F32)
    l = jnp.zeros((1, r), F32)
    acc = jnp.zeros((LANES, r), F32)
    s_next = scores(0)
    for t in range(len(tiles)):
        s = s_next
        if t + 1 < len(tiles):
            s_next = scores(t + 1)
        _, vr, rows = tiles[t]
        m_new = jnp.maximum(m, jnp.max(s, axis=0, keepdims=True))
        alpha = jnp.exp2(m - m_new)
        p = jnp.exp2(s - m_new)
        l = alpha * l + jnp.sum(p, axis=0, keepdims=True)
        pv = lax.dot_general(vr[rows, :], p.astype(BF16), _TN, preferred_element_type=F32)
        acc = alpha * acc + pv
        m = m_new
    out_t = (acc * (1.0 / l)).T
    for g in range(group):
        o_ref[:, g * LANES:(g + 1) * LANES] = out_t[g * tq:(g + 1) * tq, :].astype(BF16)


def _attention(q, kc, vc, k_lat, v_lat, *, n_kv, group, tk):
    nq = q.shape[0]
    lc = kc.shape[0]
    tq = _pick(nq, (256, 128))
    gw = group * LANES
    ctx_spec = pl.BlockSpec((lc, LANES), lambda h, i: (0, h))
    in_specs = [pl.BlockSpec((tq, gw), lambda h, i: (i, h)), ctx_spec, ctx_spec]
    args = [q, kc, vc]
    n_lat = 0
    if k_lat is not None:
        n_lat = k_lat.shape[0] // tk
        lat_spec = pl.BlockSpec((k_lat.shape[0], LANES), lambda h, i: (0, h))
        in_specs += [lat_spec, lat_spec]
        args += [k_lat, v_lat]
    r = group * tq
    est = (4 * k_lat.shape[0] * LANES * 2 * 2 if n_lat else 0) + 6 * max(tk, lc) * r * 4 + (8 << 20)
    return pl.pallas_call(
        functools.partial(_attn_kernel, tq=tq, tk=tk, n_lat=n_lat, group=group),
        grid=(n_kv, nq // tq),
        in_specs=in_specs,
        out_specs=pl.BlockSpec((tq, gw), lambda h, i: (i, h)),
        out_shape=jax.ShapeDtypeStruct((nq, n_kv * gw), BF16),
        compiler_params=_params(est, 2),
        name="attention",
    )(*args)


def _rope_tables(n, identity):
    if identity:
        return jnp.ones((n, LANES), F32), jnp.zeros((n, LANES), F32), jnp.zeros((n, LANES), F32)
    rows = n // GRID_W
    row = jnp.repeat(jnp.arange(rows, dtype=F32), GRID_W)
    col = jnp.tile(jnp.arange(GRID_W, dtype=F32), rows)
    half = LANES // 2
    inv_freq = ROPE_THETA ** (-jnp.arange(0, half, 2, dtype=F32) / half)
    ang = jnp.concatenate([row[:, None] * inv_freq, col[:, None] * inv_freq], axis=-1)
    cos, sin = jnp.cos(ang), jnp.sin(ang)
    cos_i = jnp.repeat(cos, 2, axis=-1)
    sin_i = jnp.repeat(sin, 2, axis=-1)
    even = (jnp.arange(LANES) % 2 == 0)[None, :]
    return cos_i, jnp.where(even, -sin_i, 0.0), jnp.where(even, 0.0, sin_i)


def _dft_tables(n):
    idx = jnp.arange(n, dtype=jnp.int32)
    jk = (idx[:, None] * idx[None, :]) % n
    ang = jk.astype(F32) * (2.0 * math.pi / n)
    return jnp.cos(ang), jnp.sin(ang)


def _dft_lhs(n):
    n0 = LANES if n % LANES == 0 and n > LANES else 1
    n1 = n // n0
    kk = jnp.arange(2 * n, dtype=jnp.int32)
    k = kk % n
    phase = jnp.where(kk >= n, 0.5 * math.pi, 0.0).astype(F32)
    j1 = jnp.arange(n1, dtype=jnp.int32)[:, None]
    j0 = jnp.arange(n0, dtype=jnp.int32)[:, None]
    coarse = ((j1 * k[None, :]) % n1).astype(F32) * (2.0 * math.pi / n1)
    fine = ((j0 * k[None, :]) % n).astype(F32) * (2.0 * math.pi / n) + phase
    ca, sa = jnp.cos(coarse)[:, None, :], jnp.sin(coarse)[:, None, :]
    cb, sb = jnp.cos(fine)[None, :, :], jnp.sin(fine)[None, :, :]
    return (ca * cb - sa * sb).astype(BF16).reshape(n, 2 * n)


def _row_tile(rows):
    return _pick(rows, (1024, 512, 256, 128))


def _residual(lhs, w, layer, x, gate, name):
    parts = list(lhs) if isinstance(lhs, (list, tuple)) else [lhs]
    rows, k_dim = parts[0].shape[0], sum(p.shape[1] for p in parts)
    d = x.shape[1]
    if k_dim > 4096:
        tn, tm = _pick(d, (512, 256, 128)), _pick(rows, (512, 256, 128))
    else:
        tn, tm = _pick(d, (1024, 512, 256, 128)), _pick(rows, (512, 256, 128))
    (out,) = _mm(
        lhs, w, layer=layer, seg_offs=[0], n_cols=d, tn=tn, tm=tm, epilogue=_epi_residual,
        extras=[(x, (tm, tn), lambda j, i: (i, j)), (gate.reshape(1, d), (1, tn), lambda j, i: (0, j))],
        outs=[(jax.ShapeDtypeStruct((rows, d), F32), (tm, tn), lambda j, i: (i, j))],
        name=name)
    return out


def _ffn(x, mod, w_up, conv_w, conv_b, w_down, layer, gate):
    rows = x.shape[0]
    d_ff = w_down.shape[1]
    tn = _pick(d_ff, (512, 256, 128))
    tm = _pick(rows, (512, 256, 128))
    nb = d_ff // tn
    h2 = _modulate_halo(x, *mod, tile=tm)
    cw = conv_w[layer]
    cb = conv_b[layer].reshape(1, 2 * d_ff)
    (act,) = _mm(
        h2, w_up, layer=layer, seg_offs=[0, d_ff], n_cols=d_ff, tn=tn, tm=tm, halo=True,
        lag=rows > tm, epilogue=functools.partial(_epi_ffn_up, tm=tm),
        extras=[(cw, (3, tn), lambda j, i: (0, j)), (cw, (3, tn), lambda j, i: (0, nb + j)),
                (cb, (1, tn), lambda j, i: (0, j)), (cb, (1, tn), lambda j, i: (0, nb + j))],
        outs=[(jax.ShapeDtypeStruct((rows, d_ff), BF16), (tm, tn), lambda j, i: (i, j))],
        name="ffn_up")
    return _residual(act, w_down, layer, x, gate, "ffn_down")


def _even_in(h, w_in, e, q_g, k_g, gn_g, w_s, bs_b, tabs, dims, *, want_q, want_gmlp):
    a_q_w, a_kv_w, b_w = dims
    rows = h.shape[0]
    tm = _row_tile(rows)
    tn = _pick(a_kv_w, (512, 256, 128))
    cos, sin_e, sin_o = tabs
    tab_specs = [(t, (tm, LANES), lambda j, i: (i, 0)) for t in (cos, sin_e, sin_o)]
    q = gm = None

    def proj(off, n_cols, epilogue, extras, name, lag=False):
        (out,) = _mm(
            h, w_in, layer=e, seg_offs=[off], n_cols=n_cols, tn=tn, tm=tm, epilogue=epilogue,
            extras=extras, lag=lag and rows > tm,
            outs=[(jax.ShapeDtypeStruct((rows, n_cols), BF16), (tm, tn), lambda j, i: (i, j))],
            name=name)
        return out

    if want_q:
        qscale = float(LANES ** -0.5 * math.log2(math.e))
        q = proj(0, a_q_w, functools.partial(_epi_qk, tn=tn, scale=qscale),
                 [(q_g.reshape(1, LANES), (1, LANES), lambda j, i: (0, 0))] + tab_specs, "q_proj",
                 lag=True)
    k = proj(a_q_w, a_kv_w, functools.partial(_epi_qk, tn=tn, scale=1.0),
             [(k_g.reshape(1, LANES), (1, LANES), lambda j, i: (0, 0))] + tab_specs, "k_proj",
             lag=True)
    v = proj(a_q_w + a_kv_w, a_kv_w, _epi_cast, [], "v_proj")
    if want_gmlp:
        tg = _pick(b_w, (256, 128))
        ng = tg // LANES
        (gm,) = _mm(
            h, w_in, layer=e, seg_offs=[a_q_w + 2 * a_kv_w, a_q_w + 2 * a_kv_w + b_w], n_cols=b_w,
            tn=tg, tm=tm, epilogue=functools.partial(_epi_gmlp, tm=tm, tn=tg),
            extras=[(gn_g.reshape(1, b_w), (1, tg), lambda j, i: (0, j)),
                    (w_s, (ng, LANES, LANES), lambda j, i: (j, 0, 0)),
                    (bs_b, (ng, LANES, LANES), lambda j, i: (j, 0, 0))],
            outs=[(jax.ShapeDtypeStruct((rows, b_w), BF16), (tm, tg), lambda j, i: (i, j))],
            name="gmlp")
    return q, k, v, gm


def _fourier(f_ab, n_rows, d_w, dft_lhs):
    tm = _pick(n_rows, (256, 128))
    tn = _pick(d_w, (512, 256, 128))
    scale = float(1.0 / math.sqrt(n_rows * LANES))
    (out,) = _mm(
        dft_lhs, f_ab.reshape(2 * n_rows, d_w), layer=0, seg_offs=[0], n_cols=d_w, tn=tn, tm=tm,
        epilogue=functools.partial(_epi_scale, scale=scale), extras=[],
        outs=[(jax.ShapeDtypeStruct((n_rows, d_w), BF16), (tm, tn), lambda j, i: (i, j))],
        name="pos_dft")
    return out


def _odd_mixer(x, mod, w_in, o, conv_w, c_w, d_w, chan_cs, dft_lhs):
    rows = x.shape[0]
    tn = _pick(c_w, (256, 128))
    tm = _pick(rows, (512, 256, 128))
    h = _modulate_halo(x, *mod, tile=tm)
    (y_c,) = _mm(
        h, w_in, layer=o, seg_offs=[0, c_w, 2 * c_w], n_cols=c_w, tn=tn, tm=tm, halo=True,
        epilogue=functools.partial(_epi_odd_conv, tm=tm),
        extras=[(conv_w[o], (3, tn), lambda j, i: (0, j))],
        outs=[(jax.ShapeDtypeStruct((rows, c_w), BF16), (tm, tn), lambda j, i: (i, j))],
        name="odd_conv")
    tf = _pick(d_w, (512, 256, 128))
    tmf = tm
    (f_ab,) = _mm(
        h, w_in, layer=o, seg_offs=[3 * c_w], n_cols=d_w, tn=tf, tm=tmf,
        lag=rows > tmf, epilogue=functools.partial(_epi_chan_dft, tn=tf),
        extras=[(chan_cs, (LANES, 2 * LANES), lambda j, i: (0, 0))],
        outs=[(jax.ShapeDtypeStruct((2, rows, d_w), BF16), (2, tmf, tf), lambda j, i: (0, i, j))],
        name="chan_dft")
    four = _fourier(f_ab, rows, d_w, dft_lhs)
    return [y_c, four]


def kernel(x, c, ctx, c_ctx, w_ada, b_ada, norm1_g, norm2_g, w_ffn_up, ffn_conv_w, ffn_conv_b,
           w_ffn_down, w_in_a, w_out_a, q_norm_g, k_norm_g, gmlp_norm_g, gmlp_w_s, gmlp_b_s,
           w_in_c, w_out_c, conv_w_c):
    assert x.shape[0] == 1 and c.shape[0] == 1 and ctx.shape[0] == 1
    n, d = x.shape[1], x.shape[2]
    lc = ctx.shape[1]
    depth = w_ada.shape[0]
    assert q_norm_g.shape[-1] == LANES and gmlp_w_s.shape[2] == LANES
    b_w = gmlp_norm_g.shape[-1]
    a_q_w = w_out_a.shape[1] - b_w
    a_kv_w = (w_in_a.shape[-1] - a_q_w - 2 * b_w) // 2
    n_kv = a_kv_w // LANES
    group = a_q_w // a_kv_w
    c_w = conv_w_c.shape[-1]
    d_w = w_in_c.shape[-1] - 3 * c_w
    dims = (a_q_w, a_kv_w, b_w)

    x = x[0]
    xc = ctx[0]
    s_vecs = jnp.stack([jax.nn.silu(c[0]), jax.nn.silu(c_ctx)])
    ada = _ada(s_vecs, w_ada, b_ada)

    def mod_vecs(layer, stream):
        return [ada[layer, stream, p * d:(p + 1) * d] for p in range(6)]

    tk = _pick(n, (1024, 512, 256, 128))
    tabs_lat = _rope_tables(n, identity=False)
    tabs_ctx = _rope_tables(lc, identity=True)
    bs_b = jnp.broadcast_to(gmlp_b_s[..., None], gmlp_b_s.shape + (LANES,))

    any_odd = depth > 1
    if any_odd:
        cc, sc = _dft_tables(LANES)
        chan_cs = jnp.concatenate([cc, sc], axis=1).astype(BF16)
        dft_lat = _dft_lhs(n)
        dft_ctx = _dft_lhs(lc)

    for i in range(depth):
        ctx_later = any(j % 2 == 0 for j in range(i + 1, depth))
        sh1, sc1, g1, sh2, sc2, g2 = mod_vecs(i, 0)
        csh1, csc1, cg1, csh2, csc2, cg2 = mod_vecs(i, 1)
        if i % 2 == 0:
            e = i // 2
            h = _modulate(x, norm1_g[i], sc1, sh1)
            hc = _modulate(xc, norm1_g[i], csc1, csh1)
            q, k, v, gm = _even_in(h, w_in_a, e, q_norm_g[e], k_norm_g[e], gmlp_norm_g[e],
                                   gmlp_w_s[e], bs_b[e], tabs_lat, dims,
                                   want_q=True, want_gmlp=True)
            qc, kc, vc, gmc = _even_in(hc, w_in_a, e, q_norm_g[e], k_norm_g[e], gmlp_norm_g[e],
                                       gmlp_w_s[e], bs_b[e], tabs_ctx, dims,
                                       want_q=ctx_later, want_gmlp=ctx_later)
            attn = _attention(q, kc, vc, k, v, n_kv=n_kv, group=group, tk=tk)
            mix = [attn, gm]
            x = _residual(mix, w_out_a, e, x, g1, "mix_out")
            if ctx_later:
                attn_c = _attention(qc, kc, vc, None, None, n_kv=n_kv, group=group, tk=tk)
                mix_c = [attn_c, gmc]
                xc = _residual(mix_c, w_out_a, e, xc, cg1, "mix_out")
        else:
            o = i // 2
            mix = _odd_mixer(x, (norm1_g[i], sc1, sh1), w_in_c, o, conv_w_c, c_w, d_w, chan_cs, dft_lat)
            x = _residual(mix, w_out_c, o, x, g1, "mix_out")
            if ctx_later:
                mix_c = _odd_mixer(xc, (norm1_g[i], csc1, csh1), w_in_c, o, conv_w_c, c_w, d_w,
                                   chan_cs, dft_ctx)
                xc = _residual(mix_c, w_out_c, o, xc, cg1, "mix_out")
        x = _ffn(x, (norm2_g[i], sc2, sh2), w_ffn_up, ffn_conv_w, ffn_conv_b, w_ffn_down, i, g2)
        if ctx_later:
            xc = _ffn(xc, (norm2_g[i], csc2, csh2), w_ffn_up, ffn_conv_w, ffn_conv_b, w_ffn_down, i, cg2)
    return x[None]
```

```python
import functools
import math

import jax
import jax.numpy as jnp
from jax import lax
from jax.experimental import pallas as pl
from jax.experimental.pallas import tpu as pltpu

BF16 = jnp.bfloat16
F32 = jnp.float32

GRID_W = 64
ROPE_THETA = 10000.0
EPS = 1e-6
LANES = 128
HALO = 16
MXU_DEPTH = 256
VMEM_CAP = 60 * 1024 * 1024
NEG_BIG = -1e30


def _pick(n, prefs):
    for p in prefs:
        if n % p == 0:
            return p
    return n


def _params(vmem_bytes, n_axes):
    return pltpu.CompilerParams(
        dimension_semantics=("arbitrary",) * n_axes,
        vmem_limit_bytes=int(min(VMEM_CAP, vmem_bytes)),
    )


def _ada_kernel(sb_ref, w_ref, b_ref, o_ref, *, tn, n_vec):
    for t in range(tn // LANES):
        sl = slice(t * LANES, (t + 1) * LANES)
        w = w_ref[:, sl]
        for r in range(n_vec):
            o_ref[r:r + 1, sl] = jnp.sum(w * sb_ref[r], axis=0, keepdims=True) + b_ref[:, sl]


def _ada(s_vecs, w_ada, b_ada):
    depth, d, n6 = w_ada.shape
    n_vec = s_vecs.shape[0]
    tn = _pick(n6, (512, 256, 128))
    sb = jnp.broadcast_to(s_vecs[:, :, None], (n_vec, d, LANES))
    return pl.pallas_call(
        functools.partial(_ada_kernel, tn=tn, n_vec=n_vec),
        grid=(depth, n6 // tn),
        in_specs=[
            pl.BlockSpec((n_vec, d, LANES), lambda l, j: (0, 0, 0)),
            pl.BlockSpec((None, d, tn), lambda l, j: (l, 0, j)),
            pl.BlockSpec((None, 1, tn), lambda l, j: (l, 0, j)),
        ],
        out_specs=pl.BlockSpec((None, n_vec, tn), lambda l, j: (l, 0, j)),
        out_shape=jax.ShapeDtypeStruct((depth, n_vec, n6), F32),
        compiler_params=_params(2 * d * tn * 4 + 4 * n_vec * d * LANES * 4 + (8 << 20), 2),
        name="ada",
    )(sb, w_ada, b_ada.reshape(depth, 1, n6))


def _modulate_kernel(x_ref, g_ref, sc_ref, sh_ref, o_ref):
    x = x_ref[...]
    ms = jnp.mean(x * x, axis=-1, keepdims=True)
    y = x * lax.rsqrt(ms + EPS)
    o_ref[...] = ((y * g_ref[...]) * (1.0 + sc_ref[...]) + sh_ref[...]).astype(BF16)


def _modulate(x, g, scale, shift):
    rows, d = x.shape
    tm = _pick(rows, (256, 128, 64, 32, 16))
    vec = pl.BlockSpec((1, d), lambda i: (0, 0))
    return pl.pallas_call(
        _modulate_kernel,
        grid=(rows // tm,),
        in_specs=[pl.BlockSpec((tm, d), lambda i: (i, 0)), vec, vec, vec],
        out_specs=pl.BlockSpec((tm, d), lambda i: (i, 0)),
        out_shape=jax.ShapeDtypeStruct((rows, d), BF16),
        compiler_params=_params(6 * tm * d * 4 + (8 << 20), 1),
        name="modulate",
    )(x, g.reshape(1, d), scale.reshape(1, d), shift.reshape(1, d))


def _modulate_halo_kernel(x_ref, above_ref, below_ref, g_ref, sc_ref, sh_ref, o_ref, *, tile, sub):
    t, s = pl.program_id(0), pl.program_id(1)
    n_t, n_s = pl.num_programs(0), pl.num_programs(1)
    a = g_ref[...] * (1.0 + sc_ref[...])
    b = sh_ref[...]

    def mod(x):
        ms = jnp.mean(x * x, axis=-1, keepdims=True)
        return ((x * lax.rsqrt(ms + EPS)) * a + b).astype(BF16)

    o_ref[pl.ds(pl.multiple_of(s * sub, sub), sub), :] = mod(x_ref[...])

    @pl.when(s == n_s - 1)
    def _():
        zero = jnp.zeros((HALO, x_ref.shape[1]), BF16)
        o_ref[tile:tile + HALO, :] = jnp.where(t < n_t - 1, mod(below_ref[...]), zero)
        o_ref[tile + HALO:tile + 2 * HALO, :] = jnp.where(t > 0, mod(above_ref[...]), zero)


def _modulate_halo(x, g, scale, shift, tile):
    rows, d = x.shape
    assert rows % tile == 0 and tile % HALO == 0
    sub = _pick(tile, (256, 128, 64, 32, 16))
    n_t, n_s = rows // tile, tile // sub
    hb, nblk = tile // HALO, rows // HALO
    vec = pl.BlockSpec((1, d), lambda t, s: (0, 0))
    return pl.pallas_call(
        functools.partial(_modulate_halo_kernel, tile=tile, sub=sub),
        grid=(n_t, n_s),
        in_specs=[
            pl.BlockSpec((sub, d), lambda t, s: (t * n_s + s, 0)),
            pl.BlockSpec((HALO, d), lambda t, s: (jnp.maximum(t * hb - 1, 0), 0)),
            pl.BlockSpec((HALO, d), lambda t, s: (jnp.minimum((t + 1) * hb, nblk - 1), 0)),
            vec, vec, vec,
        ],
        out_specs=pl.BlockSpec((None, tile + 2 * HALO, d), lambda t, s: (t, 0, 0)),
        out_shape=jax.ShapeDtypeStruct((n_t, tile + 2 * HALO, d), BF16),
        compiler_params=_params(4 * sub * d * 4 + 2 * (tile + 2 * HALO) * d * 2 + (8 << 20), 2),
        name="modulate_halo",
    )(x, x, x, g.reshape(1, d), scale.reshape(1, d), shift.reshape(1, d))


def _mm_kernel(*refs, n_seg, k_parts, halo, cast_w, lag, n_tiles, tm, epilogue, n_extra, n_out,
               k_dim):
    it = iter(refs)
    lhs_refs = [next(it) for _ in k_parts]
    w_refs = [next(it) for _ in range(n_seg)]
    extra_refs = [next(it) for _ in range(n_extra)]
    out_refs = [next(it) for _ in range(n_out)]
    wb_refs = [next(it) for _ in range(n_seg)] if cast_w else w_refs
    n_slots = 2 if lag else (1 if halo else 0)
    acc_slots = [[next(it) for _ in range(n_seg)] for _ in range(n_slots)]

    i = pl.program_id(1)
    if cast_w:
        ck = _pick(k_dim, (256, 128, 64, 32, 16))

        @pl.when(i == 0)
        def _():
            def cast_chunk(c, carry):
                rows = pl.ds(pl.multiple_of(c * ck, ck), ck)
                for w_ref, wb_ref in zip(w_refs, wb_refs):
                    wb_ref[rows, :] = w_ref[rows, :].astype(BF16)
                return carry
            lax.fori_loop(0, k_dim // ck, cast_chunk, 0)

    def dots(between=()):
        lhs_vals = [r[...] for r in lhs_refs]
        if between:
            kc = k_dim // len(between)
            accs = [None] * n_seg
            for c, epi_chunk in enumerate(between):
                for s, wb_ref in enumerate(wb_refs):
                    part = jnp.dot(lhs_vals[0][:, c * kc:(c + 1) * kc], wb_ref[c * kc:(c + 1) * kc, :],
                                   preferred_element_type=F32)
                    accs[s] = part if accs[s] is None else accs[s] + part
                epi_chunk()
            return accs
        accs = []
        for wb_ref in wb_refs:
            acc, k0 = None, 0
            for lhs, kp in zip(lhs_vals, k_parts):
                part = jnp.dot(lhs, wb_ref[k0:k0 + kp, :], preferred_element_type=F32)
                acc = part if acc is None else acc + part
                k0 += kp
            accs.append(acc)
        return accs

    def store(slot, between=()):
        for acc, ref in zip(dots(between), acc_slots[slot]):
            if halo:
                ref[HALO:, :] = acc[:tm + HALO, :]
                ref[:HALO, :] = acc[tm + HALO:, :]
            else:
                ref[...] = acc

    def finish(slot):
        accs = acc_slots[slot] if halo else [ref[...] for ref in acc_slots[slot]]
        epilogue(accs, extra_refs, out_refs)

    if not lag:
        if halo:
            store(0)
            finish(0)
        else:
            epilogue(dots(), extra_refs, out_refs)
        return


    @pl.when(i == 0)
    def _():
        store(0)

    for parity in range(2):
        @pl.when(jnp.logical_and(jnp.logical_and(i > 0, i < n_tiles), i % 2 == parity))
        def _():
            if halo:
                nr = max(8, tm // max(1, k_dim // MXU_DEPTH))
                store(parity, [functools.partial(epilogue, acc_slots[1 - parity], extra_refs, out_refs,
                                                 rows=(r0, nr)) for r0 in range(0, tm, nr)])
            else:
                finish(1 - parity)
                store(parity)

    @pl.when(i == n_tiles)
    def _():
        finish((n_tiles - 1) % 2)


def _mm(lhs, w, *, layer, seg_offs, n_cols, tn, tm, epilogue, extras, outs, halo=False, lag=False,
        name="mm"):
    lhs_parts = list(lhs) if isinstance(lhs, (list, tuple)) else [lhs]
    tiled = lhs_parts[0].ndim == 3
    if tiled:
        assert len(lhs_parts) == 1 and lhs_parts[0].shape[1] == tm + 2 * HALO
        rows = lhs_parts[0].shape[0] * tm
        k_parts = (lhs_parts[0].shape[2],)
    else:
        assert not halo
        rows = lhs_parts[0].shape[0]
        k_parts = tuple(p.shape[1] for p in lhs_parts)
    k_dim = sum(k_parts)
    assert rows % tm == 0 and n_cols % tn == 0
    n_tiles = rows // tm
    cast_w = w.dtype != BF16
    if lag:
        def dot_tile(i):
            return jnp.minimum(i, n_tiles - 1)

        def epi_tile(i):
            return jnp.maximum(i - 1, 0)
    else:
        def dot_tile(i):
            return i
        epi_tile = dot_tile
    if tiled:
        in_specs = [pl.BlockSpec((None, tm + 2 * HALO if halo else tm, k_dim),
                                 lambda j, i: (dot_tile(i), 0, 0))]
    else:
        in_specs = [pl.BlockSpec((tm, kp), lambda j, i: (dot_tile(i), 0)) for kp in k_parts]
    args = list(lhs_parts)
    for off in seg_offs:
        assert off % tn == 0
        ob = off // tn
        if w.ndim == 3:
            in_specs.append(pl.BlockSpec((None, k_dim, tn), lambda j, i, ob=ob: (layer, 0, ob + j)))
        else:
            in_specs.append(pl.BlockSpec((k_dim, tn), lambda j, i, ob=ob: (0, ob + j)))
        args.append(w)
    for arr, bs, im in extras:
        in_specs.append(pl.BlockSpec(bs, lambda j, i, im=im: im(j, epi_tile(i))))
        args.append(arr)
    out_specs = [pl.BlockSpec(bs, lambda j, i, im=im: im(j, epi_tile(i))) for _, bs, im in outs]
    out_shape = [sds for sds, _, _ in outs]
    n_seg = len(seg_offs)
    rows_l = tm + 2 * HALO if halo else tm
    scratch = []
    if cast_w:
        scratch += [pltpu.VMEM((k_dim, tn), BF16) for _ in range(n_seg)]
    n_slots = 2 if lag else (1 if halo else 0)
    scratch += [pltpu.VMEM((rows_l, tn), F32) for _ in range(n_slots * n_seg)]
    wbytes = w.dtype.itemsize
    est = (n_seg * k_dim * tn * (2 * wbytes + (2 if cast_w else 0))
           + 2 * rows_l * k_dim * 2 + (5 + 4 * n_seg) * rows_l * tn * 4 + (8 << 20))
    kern = functools.partial(
        _mm_kernel, n_seg=n_seg, k_parts=k_parts, halo=halo, cast_w=cast_w, lag=lag, n_tiles=n_tiles,
        tm=tm, epilogue=epilogue, n_extra=len(extras), n_out=len(outs), k_dim=k_dim)
    return pl.pallas_call(
        kern,
        grid=(n_cols // tn, n_tiles + (1 if lag else 0)),
        in_specs=in_specs,
        out_specs=out_specs,
        out_shape=out_shape,
        scratch_shapes=scratch,
        compiler_params=_params(est, 2),
        name=name,
    )(*args)


def _rms_lanes(x):
    return x * lax.rsqrt(jnp.mean(x * x, axis=-1, keepdims=True) + EPS)


def _epi_qk(accs, ex, outs, *, tn, scale):
    (acc,) = accs
    g_ref, cos_ref, sin_e_ref, sin_o_ref = ex
    (o_ref,) = outs
    g = g_ref[...]
    cos, sin_e, sin_o = cos_ref[...], sin_e_ref[...], sin_o_ref[...]
    for h in range(tn // LANES):
        hs = slice(h * LANES, (h + 1) * LANES)
        y = _rms_lanes(acc[:, hs]) * g
        r = y * cos + pltpu.roll(y, LANES - 1, 1) * sin_e + pltpu.roll(y, 1, 1) * sin_o
        o_ref[:, hs] = (r * scale if scale != 1.0 else r).astype(BF16)


def _epi_cast(accs, ex, outs):
    (acc,) = accs
    (o_ref,) = outs
    o_ref[...] = acc.astype(BF16)


def _epi_gmlp(accs, ex, outs, *, tm, tn):
    u, gv = accs
    gn_ref, ws_ref, bs_ref = ex
    (o_ref,) = outs
    gu = jax.nn.gelu(u)
    ggv = jax.nn.gelu(gv)
    nc = tm // LANES
    for g in range(tn // LANES):
        gs = slice(g * LANES, (g + 1) * LANES)
        vn = (_rms_lanes(ggv[:, gs]) * gn_ref[:, gs]).astype(BF16)
        ws = ws_ref[g].astype(BF16)
        bias = bs_ref[g]
        for c in range(nc):
            cs = slice(c * LANES, (c + 1) * LANES)
            res = jnp.dot(ws, vn[cs, :], preferred_element_type=F32)
            o_ref[cs, gs] = (gu[cs, gs] * (res + bias)).astype(BF16)


CONV_ROWS = 64


def _row_windows(ref, r0, nr):
    base = HALO + r0
    return ref[base - 1:base - 1 + nr, :], ref[base:base + nr, :], ref[base + 1:base + 1 + nr, :]


def _conv_chunks(tm, rows):
    nr = min(CONV_ROWS, tm)
    return [rows] if rows is not None else [(r0, nr) for r0 in range(0, tm, nr)]


def _epi_ffn_up(acc_refs, ex, outs, *, tm, rows=None):
    g_ref, u_ref = acc_refs
    cwg_ref, cwu_ref, cbg_ref, cbu_ref = ex
    (o_ref,) = outs
    for r0, nr in _conv_chunks(tm, rows):
        gp, gm, gn = _row_windows(g_ref, r0, nr)
        up, um, un = _row_windows(u_ref, r0, nr)
        ag = gp * cwg_ref[0:1, :] + gm * cwg_ref[1:2, :] + gn * cwg_ref[2:3, :] + cbg_ref[...]
        au = up * cwu_ref[0:1, :] + um * cwu_ref[1:2, :] + un * cwu_ref[2:3, :] + cbu_ref[...]
        o_ref[r0:r0 + nr, :] = (jax.nn.silu(ag) * au).astype(BF16)


def _epi_odd_conv(acc_refs, ex, outs, *, tm, rows=None):
    bg_ref, cg_ref, hx_ref = acc_refs
    (cw_ref,) = ex
    (o_ref,) = outs
    for r0, nr in _conv_chunks(tm, rows):
        cp, cm, cn = _row_windows(cg_ref, r0, nr)
        hp, hm, hn = _row_windows(hx_ref, r0, nr)
        conv = (cp * hp) * cw_ref[0:1, :] + (cm * hm) * cw_ref[1:2, :] + (cn * hn) * cw_ref[2:3, :]
        o_ref[r0:r0 + nr, :] = (bg_ref[HALO + r0:HALO + r0 + nr, :] * conv).astype(BF16)


def _epi_residual(accs, ex, outs):
    (acc,) = accs
    x_ref, gate_ref = ex
    (o_ref,) = outs
    o_ref[...] = x_ref[...] + gate_ref[...] * acc


def _epi_chan_dft(accs, ex, outs, *, tn):
    (acc,) = accs
    (cs_ref,) = ex
    (o_ref,) = outs
    cs = cs_ref[...]
    for g in range(tn // LANES):
        gs = slice(g * LANES, (g + 1) * LANES)
        ab = jnp.dot(acc[:, gs].astype(BF16), cs, preferred_element_type=F32)
        o_ref[0, :, gs] = ab[:, :LANES].astype(BF16)
        o_ref[1, :, gs] = ab[:, LANES:].astype(BF16)


def _epi_scale(accs, ex, outs, *, scale):
    (acc,) = accs
    (o_ref,) = outs
    o_ref[...] = (acc * scale).astype(BF16)


_NT = (((1,), (1,)), ((), ()))
_TN = (((0,), (0,)), ((), ()))


def _attn_kernel(*refs, tq, tk, n_lat, group):
    if n_lat:
        q_ref, kc_ref, vc_ref, k_ref, v_ref, o_ref = refs
    else:
        q_ref, kc_ref, vc_ref, o_ref = refs
    r = group * tq
    q = jnp.concatenate([q_ref[:, g * LANES:(g + 1) * LANES] for g in range(group)], axis=0)
    tiles = [(kc_ref, vc_ref, slice(None))]
    tiles += [(k_ref, v_ref, slice(t * tk, (t + 1) * tk)) for t in range(n_lat)]

    def scores(t):
        kr, _, rows = tiles[t]
        return lax.dot_general(kr[rows, :], q, _NT, preferred_element_type=F32)

    m = jnp.full((1, r), NEG_BIG, F32)
    l = jnp.zeros((1, r), F32)
    acc = jnp.zeros((LANES, r), F32)
    s_next = scores(0)
    for t in range(len(tiles)):
        s = s_next
        if t + 1 < len(tiles):
            s_next = scores(t + 1)
        _, vr, rows = tiles[t]
        m_new = jnp.maximum(m, jnp.max(s, axis=0, keepdims=True))
        alpha = jnp.exp2(m - m_new)
        p = jnp.exp2(s - m_new)
        l = alpha * l + jnp.sum(p, axis=0, keepdims=True)
        pv = lax.dot_general(vr[rows, :], p.astype(BF16), _TN, preferred_element_type=F32)
        acc = alpha * acc + pv
        m = m_new
    out_t = (acc * (1.0 / l)).T
    for g in range(group):
        o_ref[:, g * LANES:(g + 1) * LANES] = out_t[g * tq:(g + 1) * tq, :].astype(BF16)


def _attention(q, kc, vc, k_lat, v_lat, *, n_kv, group, tk):
    nq = q.shape[0]
    lc = kc.shape[0]
    tq = _pick(nq, (256, 128))
    gw = group * LANES
    ctx_spec = pl.BlockSpec((lc, LANES), lambda h, i: (0, h))
    in_specs = [pl.BlockSpec((tq, gw), lambda h, i: (i, h)), ctx_spec, ctx_spec]
    args = [q, kc, vc]
    n_lat = 0
    if k_lat is not None:
        n_lat = k_lat.shape[0] // tk
        lat_spec = pl.BlockSpec((k_lat.shape[0], LANES), lambda h, i: (0, h))
        in_specs += [lat_spec, lat_spec]
        args += [k_lat, v_lat]
    r = group * tq
    est = (4 * k_lat.shape[0] * LANES * 2 * 2 if n_lat else 0) + 6 * max(tk, lc) * r * 4 + (8 << 20)
    return pl.pallas_call(
        functools.partial(_attn_kernel, tq=tq, tk=tk, n_lat=n_lat, group=group),
        grid=(n_kv, nq // tq),
        in_specs=in_specs,
        out_specs=pl.BlockSpec((tq, gw), lambda h, i: (i, h)),
        out_shape=jax.ShapeDtypeStruct((nq, n_kv * gw), BF16),
        compiler_params=_params(est, 2),
        name="attention",
    )(*args)


def _rope_tables(n, identity):
    if identity:
        return jnp.ones((n, LANES), F32), jnp.zeros((n, LANES), F32), jnp.zeros((n, LANES), F32)
    rows = n // GRID_W
    row = jnp.repeat(jnp.arange(rows, dtype=F32), GRID_W)
    col = jnp.tile(jnp.arange(GRID_W, dtype=F32), rows)
    half = LANES // 2
    inv_freq = ROPE_THETA ** (-jnp.arange(0, half, 2, dtype=F32) / half)
    ang = jnp.concatenate([row[:, None] * inv_freq, col[:, None] * inv_freq], axis=-1)
    cos, sin = jnp.cos(ang), jnp.sin(ang)
    cos_i = jnp.repeat(cos, 2, axis=-1)
    sin_i = jnp.repeat(sin, 2, axis=-1)
    even = (jnp.arange(LANES) % 2 == 0)[None, :]
    return cos_i, jnp.where(even, -sin_i, 0.0), jnp.where(even, 0.0, sin_i)


def _dft_tables(n):
    idx = jnp.arange(n, dtype=jnp.int32)
    jk = (idx[:, None] * idx[None, :]) % n
    ang = jk.astype(F32) * (2.0 * math.pi / n)
    return jnp.cos(ang), jnp.sin(ang)


def _dft_lhs(n):
    n0 = LANES if n % LANES == 0 and n > LANES else 1
    n1 = n // n0
    kk = jnp.arange(2 * n, dtype=jnp.int32)
    k = kk % n
    phase = jnp.where(kk >= n, 0.5 * math.pi, 0.0).astype(F32)
    j1 = jnp.arange(n1, dtype=jnp.int32)[:, None]
    j0 = jnp.arange(n0, dtype=jnp.int32)[:, None]
    coarse = ((j1 * k[None, :]) % n1).astype(F32) * (2.0 * math.pi / n1)
    fine = ((j0 * k[None, :]) % n).astype(F32) * (2.0 * math.pi / n) + phase
    ca, sa = jnp.cos(coarse)[:, None, :], jnp.sin(coarse)[:, None, :]
    cb, sb = jnp.cos(fine)[None, :, :], jnp.sin(fine)[None, :, :]
    return (ca * cb - sa * sb).astype(BF16).reshape(n, 2 * n)


def _row_tile(rows):
    return _pick(rows, (1024, 512, 256, 128))


def _residual(lhs, w, layer, x, gate, name):
    parts = list(lhs) if isinstance(lhs, (list, tuple)) else [lhs]
    rows, k_dim = parts[0].shape[0], sum(p.shape[1] for p in parts)
    d = x.shape[1]
    if k_dim > 4096:
        tn, tm = _pick(d, (512, 256, 128)), _pick(rows, (512, 256, 128))
    else:
        tn, tm = _pick(d, (1024, 512, 256, 128)), _pick(rows, (512, 256, 128))
    (out,) = _mm(
        lhs, w, layer=layer, seg_offs=[0], n_cols=d, tn=tn, tm=tm, epilogue=_epi_residual,
        extras=[(x, (tm, tn), lambda j, i: (i, j)), (gate.reshape(1, d), (1, tn), lambda j, i: (0, j))],
        outs=[(jax.ShapeDtypeStruct((rows, d), F32), (tm, tn), lambda j, i: (i, j))],
        name=name)
    return out


def _ffn(x, mod, w_up, conv_w, conv_b, w_down, layer, gate):
    rows = x.shape[0]
    d_ff = w_down.shape[1]
    tn = _pick(d_ff, (512, 256, 128))
    tm = _pick(rows, (512, 256, 128))
    nb = d_ff // tn
    h2 = _modulate_halo(x, *mod, tile=tm)
    cw = conv_w[layer]
    cb = conv_b[layer].reshape(1, 2 * d_ff)
    (act,) = _mm(
        h2, w_up, layer=layer, seg_offs=[0, d_ff], n_cols=d_ff, tn=tn, tm=tm, halo=True,
        lag=rows > tm, epilogue=functools.partial(_epi_ffn_up, tm=tm),
        extras=[(cw, (3, tn), lambda j, i: (0, j)), (cw, (3, tn), lambda j, i: (0, nb + j)),
                (cb, (1, tn), lambda j, i: (0, j)), (cb, (1, tn), lambda j, i: (0, nb + j))],
        outs=[(jax.ShapeDtypeStruct((rows, d_ff), BF16), (tm, tn), lambda j, i: (i, j))],
        name="ffn_up")
    return _residual(act, w_down, layer, x, gate, "ffn_down")


def _even_in(h, w_in, e, q_g, k_g, gn_g, w_s, bs_b, tabs, dims, *, want_q, want_gmlp):
    a_q_w, a_kv_w, b_w = dims
    rows = h.shape[0]
    tm = _row_tile(rows)
    tn = _pick(a_kv_w, (512, 256, 128))
    cos, sin_e, sin_o = tabs
    tab_specs = [(t, (tm, LANES), lambda j, i: (i, 0)) for t in (cos, sin_e, sin_o)]
    q = gm = None

    def proj(off, n_cols, epilogue, extras, name, lag=False):
        (out,) = _mm(
            h, w_in, layer=e, seg_offs=[off], n_cols=n_cols, tn=tn, tm=tm, epilogue=epilogue,
            extras=extras, lag=lag and rows > tm,
            outs=[(jax.ShapeDtypeStruct((rows, n_cols), BF16), (tm, tn), lambda j, i: (i, j))],
            name=name)
        return out

    if want_q:
        qscale = float(LANES ** -0.5 * math.log2(math.e))
        q = proj(0, a_q_w, functools.partial(_epi_qk, tn=tn, scale=qscale),
                 [(q_g.reshape(1, LANES), (1, LANES), lambda j, i: (0, 0))] + tab_specs, "q_proj",
                 lag=True)
    k = proj(a_q_w, a_kv_w, functools.partial(_epi_qk, tn=tn, scale=1.0),
             [(k_g.reshape(1, LANES), (1, LANES), lambda j, i: (0, 0))] + tab_specs, "k_proj",
             lag=True)
    v = proj(a_q_w + a_kv_w, a_kv_w, _epi_cast, [], "v_proj")
    if want_gmlp:
        tg = _pick(b_w, (256, 128))
        ng = tg // LANES
        (gm,) = _mm(
            h, w_in, layer=e, seg_offs=[a_q_w + 2 * a_kv_w, a_q_w + 2 * a_kv_w + b_w], n_cols=b_w,
            tn=tg, tm=tm, epilogue=functools.partial(_epi_gmlp, tm=tm, tn=tg),
            extras=[(gn_g.reshape(1, b_w), (1, tg), lambda j, i: (0, j)),
                    (w_s, (ng, LANES, LANES), lambda j, i: (j, 0, 0)),
                    (bs_b, (ng, LANES, LANES), lambda j, i: (j, 0, 0))],
            outs=[(jax.ShapeDtypeStruct((rows, b_w), BF16), (tm, tg), lambda j, i: (i, j))],
            name="gmlp")
    return q, k, v, gm


def _epi_f32(accs, ex, outs):
    (acc,) = accs
    (o_ref,) = outs
    o_ref[...] = acc


FFT_GROUP = 8


def _fft_twiddle_kernel(y_ref, ct_ref, st_ref, o_ref, *, n2, ch):
    for g in range(FFT_GROUP):
        ct, st = ct_ref[g], st_ref[g]
        for c in range(ch // LANES):
            cols = slice(g * ch + c * LANES, g * ch + (c + 1) * LANES)
            sl = slice(c * LANES, (c + 1) * LANES)
            yr, yi = y_ref[:n2, cols], y_ref[n2:, cols]
            o_ref[0, g, :, sl] = (yr * ct - yi * st).astype(BF16)
            o_ref[1, g, :, sl] = (-(yi * ct) - yr * st).astype(BF16)


def _fourier_two_stage(f_ab, n_rows, d_w):
    n2 = LANES
    n1 = n_rows // n2
    assert n1 % FFT_GROUP == 0
    c2, s2 = _dft_tables(n2)
    m2 = jnp.concatenate([jnp.concatenate([c2, -s2], axis=1), jnp.concatenate([s2, c2], axis=1)],
                         axis=0).astype(BF16)
    tn = _pick(n1 * d_w, (2048, 1024, 512, 256, 128))
    (y,) = _mm(
        m2, f_ab.reshape(2 * n2, n1 * d_w), layer=0, seg_offs=[0], n_cols=n1 * d_w, tn=tn, tm=2 * n2,
        epilogue=_epi_f32, extras=[],
        outs=[(jax.ShapeDtypeStruct((2 * n2, n1 * d_w), F32), (2 * n2, tn), lambda j, i: (i, j))],
        name="fft_stage1")
    p1 = jnp.arange(n1, dtype=jnp.int32)[:, None]
    k2 = jnp.arange(n2, dtype=jnp.int32)[None, :]
    ang = ((p1 * k2) % n_rows).astype(F32) * (2.0 * math.pi / n_rows)
    ct = jnp.broadcast_to(jnp.cos(ang)[:, :, None], (n1, n2, LANES))
    st = jnp.broadcast_to(jnp.sin(ang)[:, :, None], (n1, n2, LANES))
    tw_spec = pl.BlockSpec((FFT_GROUP, n2, LANES), lambda a: (a, 0, 0))
    t = pl.pallas_call(
        functools.partial(_fft_twiddle_kernel, n2=n2, ch=d_w),
        grid=(n1 // FFT_GROUP,),
        in_specs=[pl.BlockSpec((2 * n2, FFT_GROUP * d_w), lambda a: (0, a)), tw_spec, tw_spec],
        out_specs=pl.BlockSpec((2, FFT_GROUP, n2, d_w), lambda a: (0, a, 0, 0)),
        out_shape=jax.ShapeDtypeStruct((2, n1, n2, d_w), BF16),
        compiler_params=_params(6 * 2 * n2 * FFT_GROUP * d_w * 4 + (8 << 20), 1),
        name="fft_twiddle",
    )(y, ct, st)
    c1, s1 = _dft_tables(n1)
    f1 = jnp.concatenate([c1, s1], axis=1).astype(BF16)
    scale = float(1.0 / math.sqrt(n_rows * LANES))
    tn3 = _pick(n2 * d_w, (4096, 2048, 1024, 512, 256, 128))
    (out,) = _mm(
        f1, t.reshape(2 * n1, n2 * d_w), layer=0, seg_offs=[0], n_cols=n2 * d_w, tn=tn3, tm=n1,
        epilogue=functools.partial(_epi_scale, scale=scale), extras=[],
        outs=[(jax.ShapeDtypeStruct((n1, n2 * d_w), BF16), (n1, tn3), lambda j, i: (i, j))],
        name="fft_stage3")
    return out.reshape(n_rows, d_w)


def _fourier(f_ab, n_rows, d_w, dft_lhs):
    if dft_lhs is None:
        return _fourier_two_stage(f_ab, n_rows, d_w)
    tm = _pick(n_rows, (256, 128))
    tn = _pick(d_w, (512, 256, 128))
    scale = float(1.0 / math.sqrt(n_rows * LANES))
    (out,) = _mm(
        dft_lhs, f_ab.reshape(2 * n_rows, d_w), layer=0, seg_offs=[0], n_cols=d_w, tn=tn, tm=tm,
        epilogue=functools.partial(_epi_scale, scale=scale), extras=[],
        outs=[(jax.ShapeDtypeStruct((n_rows, d_w), BF16), (tm, tn), lambda j, i: (i, j))],
        name="pos_dft")
    return out


def _odd_mixer(x, mod, w_in, o, conv_w, c_w, d_w, chan_cs, dft_lhs):
    rows = x.shape[0]
    tn = _pick(c_w, (256, 128))
    tm = _pick(rows, (512, 256, 128))
    h = _modulate_halo(x, *mod, tile=tm)
    (y_c,) = _mm(
        h, w_in, layer=o, seg_offs=[0, c_w, 2 * c_w], n_cols=c_w, tn=tn, tm=tm, halo=True,
        epilogue=functools.partial(_epi_odd_conv, tm=tm),
        extras=[(conv_w[o], (3, tn), lambda j, i: (0, j))],
        outs=[(jax.ShapeDtypeStruct((rows, c_w), BF16), (tm, tn), lambda j, i: (i, j))],
        name="odd_conv")
    tf = _pick(d_w, (512, 256, 128))
    tmf = tm
    (f_ab,) = _mm(
        h, w_in, layer=o, seg_offs=[3 * c_w], n_cols=d_w, tn=tf, tm=tmf,
        lag=rows > tmf, epilogue=functools.partial(_epi_chan_dft, tn=tf),
        extras=[(chan_cs, (LANES, 2 * LANES), lambda j, i: (0, 0))],
        outs=[(jax.ShapeDtypeStruct((2, rows, d_w), BF16), (2, tmf, tf), lambda j, i: (0, i, j))],
        name="chan_dft")
    four = _fourier(f_ab, rows, d_w, dft_lhs)
    return [y_c, four]


def kernel(x, c, ctx, c_ctx, w_ada, b_ada, norm1_g, norm2_g, w_ffn_up, ffn_conv_w, ffn_conv_b,
           w_ffn_down, w_in_a, w_out_a, q_norm_g, k_norm_g, gmlp_norm_g, gmlp_w_s, gmlp_b_s,
           w_in_c, w_out_c, conv_w_c):
    assert x.shape[0] == 1 and c.shape[0] == 1 and ctx.shape[0] == 1
    n, d = x.shape[1], x.shape[2]
    lc = ctx.shape[1]
    depth = w_ada.shape[0]
    assert q_norm_g.shape[-1] == LANES and gmlp_w_s.shape[2] == LANES
    b_w = gmlp_norm_g.shape[-1]
    a_q_w = w_out_a.shape[1] - b_w
    a_kv_w = (w_in_a.shape[-1] - a_q_w - 2 * b_w) // 2
    n_kv = a_kv_w // LANES
    group = a_q_w // a_kv_w
    c_w = conv_w_c.shape[-1]
    d_w = w_in_c.shape[-1] - 3 * c_w
    dims = (a_q_w, a_kv_w, b_w)

    x = x[0]
    xc = ctx[0]
    s_vecs = jnp.stack([jax.nn.silu(c[0]), jax.nn.silu(c_ctx)])
    ada = _ada(s_vecs, w_ada, b_ada)

    def mod_vecs(layer, stream):
        return [ada[layer, stream, p * d:(p + 1) * d] for p in range(6)]

    tk = _pick(n, (1024, 512, 256, 128))
    tabs_lat = _rope_tables(n, identity=False)
    tabs_ctx = _rope_tables(lc, identity=True)
    bs_b = jnp.broadcast_to(gmlp_b_s[..., None], gmlp_b_s.shape + (LANES,))

    any_odd = depth > 1
    if any_odd:
        cc, sc = _dft_tables(LANES)
        chan_cs = jnp.concatenate([cc, sc], axis=1).astype(BF16)
        dft_lat = None if n >= 16 * LANES else _dft_lhs(n)
        dft_ctx = None if lc >= 16 * LANES else _dft_lhs(lc)

    for i in range(depth):
        ctx_later = any(j % 2 == 0 for j in range(i + 1, depth))
        sh1, sc1, g1, sh2, sc2, g2 = mod_vecs(i, 0)
        csh1, csc1, cg1, csh2, csc2, cg2 = mod_vecs(i, 1)
        if i % 2 == 0:
            e = i // 2
            h = _modulate(x, norm1_g[i], sc1, sh1)
            hc = _modulate(xc, norm1_g[i], csc1, csh1)
            q, k, v, gm = _even_in(h, w_in_a, e, q_norm_g[e], k_norm_g[e], gmlp_norm_g[e],
                                   gmlp_w_s[e], bs_b[e], tabs_lat, dims,
                                   want_q=True, want_gmlp=True)
            qc, kc, vc, gmc = _even_in(hc, w_in_a, e, q_norm_g[e], k_norm_g[e], gmlp_norm_g[e],
                                       gmlp_w_s[e], bs_b[e], tabs_ctx, dims,
                                       want_q=ctx_later, want_gmlp=ctx_later)
            attn = _attention(q, kc, vc, k, v, n_kv=n_kv, group=group, tk=tk)
            mix = [attn, gm]
            x = _residual(mix, w_out_a, e, x, g1, "mix_out")
            if ctx_later:
                attn_c = _attention(qc, kc, vc, None, None, n_kv=n_kv, group=group, tk=tk)
                mix_c = [attn_c, gmc]
                xc = _residual(mix_c, w_out_a, e, xc, cg1, "mix_out")
        else:
            o = i // 2
            mix = _odd_mixer(x, (norm1_g[i], sc1, sh1), w_in_c, o, conv_w_c, c_w, d_w, chan_cs, dft_lat)
            x = _residual(mix, w_out_c, o, x, g1, "mix_out")
            if ctx_later:
                mix_c = _odd_mixer(xc, (norm1_g[i], csc1, csh1), w_in_c, o, conv_w_c, c_w, d_w,
                                   chan_cs, dft_ctx)
                xc = _residual(mix_c, w_out_c, o, xc, cg1, "mix_out")
        x = _ffn(x, (norm2_g[i], sc2, sh2), w_ffn_up, ffn_conv_w, ffn_conv_b, w_ffn_down, i, g2)
        if ctx_later:
            xc = _ffn(xc, (norm2_g[i], csc2, csh2), w_ffn_up, ffn_conv_w, ffn_conv_b, w_ffn_down, i, cg2)
    return x[None]
```

```python
import functools
import math

import jax
import jax.numpy as jnp
from jax import lax
from jax.experimental import pallas as pl
from jax.experimental.pallas import tpu as pltpu

BF16 = jnp.bfloat16
F32 = jnp.float32

GRID_W = 64
ROPE_THETA = 10000.0
EPS = 1e-6
LANES = 128
HALO = 8
MXU_DEPTH = 256
VMEM_CAP = 60 * 1024 * 1024
NEG_BIG = -1e30


def _pick(n, prefs):
    for p in prefs:
        if n % p == 0:
            return p
    return n


def _params(vmem_bytes, n_axes):
    return pltpu.CompilerParams(
        dimension_semantics=("arbitrary",) * n_axes,
        vmem_limit_bytes=int(min(VMEM_CAP, vmem_bytes)),
    )


def _ada_kernel(sb_ref, w_ref, b_ref, o_ref, *, tn, n_vec):
    for t in range(tn // LANES):
        sl = slice(t * LANES, (t + 1) * LANES)
        w = w_ref[:, sl]
        for r in range(n_vec):
            o_ref[r:r + 1, sl] = jnp.sum(w * sb_ref[r], axis=0, keepdims=True) + b_ref[:, sl]


def _ada(s_vecs, w_ada, b_ada):
    depth, d, n6 = w_ada.shape
    n_vec = s_vecs.shape[0]
    tn = _pick(n6, (512, 256, 128))
    sb = jnp.broadcast_to(s_vecs[:, :, None], (n_vec, d, LANES))
    return pl.pallas_call(
        functools.partial(_ada_kernel, tn=tn, n_vec=n_vec),
        grid=(depth, n6 // tn),
        in_specs=[
            pl.BlockSpec((n_vec, d, LANES), lambda l, j: (0, 0, 0)),
            pl.BlockSpec((None, d, tn), lambda l, j: (l, 0, j)),
            pl.BlockSpec((None, 1, tn), lambda l, j: (l, 0, j)),
        ],
        out_specs=pl.BlockSpec((None, n_vec, tn), lambda l, j: (l, 0, j)),
        out_shape=jax.ShapeDtypeStruct((depth, n_vec, n6), F32),
        compiler_params=_params(2 * d * tn * 4 + 4 * n_vec * d * LANES * 4 + (8 << 20), 2),
        name="ada",
    )(sb, w_ada, b_ada.reshape(depth, 1, n6))


def _modulate_kernel(x_ref, g_ref, sc_ref, sh_ref, o_ref):
    x = x_ref[...]
    ms = jnp.mean(x * x, axis=-1, keepdims=True)
    y = x * lax.rsqrt(ms + EPS)
    o_ref[...] = ((y * g_ref[...]) * (1.0 + sc_ref[...]) + sh_ref[...]).astype(BF16)


def _modulate(x, g, scale, shift):
    rows, d = x.shape
    tm = _pick(rows, (256, 128, 64, 32, 16))
    vec = pl.BlockSpec((1, d), lambda i: (0, 0))
    return pl.pallas_call(
        _modulate_kernel,
        grid=(rows // tm,),
        in_specs=[pl.BlockSpec((tm, d), lambda i: (i, 0)), vec, vec, vec],
        out_specs=pl.BlockSpec((tm, d), lambda i: (i, 0)),
        out_shape=jax.ShapeDtypeStruct((rows, d), BF16),
        compiler_params=_params(6 * tm * d * 4 + (8 << 20), 1),
        name="modulate",
    )(x, g.reshape(1, d), scale.reshape(1, d), shift.reshape(1, d))


def _modulate_halo_kernel(x_ref, above_ref, below_ref, g_ref, sc_ref, sh_ref, o_ref, *, tile, sub):
    t, s = pl.program_id(0), pl.program_id(1)
    n_t, n_s = pl.num_programs(0), pl.num_programs(1)
    a = g_ref[...] * (1.0 + sc_ref[...])
    b = sh_ref[...]

    def mod(x):
        ms = jnp.mean(x * x, axis=-1, keepdims=True)
        return (x * lax.rsqrt(ms + EPS)) * a + b

    o_ref[pl.ds(pl.multiple_of(s * sub, sub), sub), :] = mod(x_ref[...]).astype(BF16)

    @pl.when(s == n_s - 1)
    def _():
        below = jnp.where(t < n_t - 1, mod(below_ref[...]), 0.0)
        above = jnp.where(t > 0, mod(above_ref[...]), 0.0)
        o_ref[tile:tile + 2 * HALO, :] = jnp.concatenate([below, above], axis=0).astype(BF16)


def _modulate_halo(x, g, scale, shift, tile):
    rows, d = x.shape
    assert rows % tile == 0 and tile % HALO == 0
    sub = _pick(tile, (256, 128, 64, 32, 16))
    n_t, n_s = rows // tile, tile // sub
    hb, nblk = tile // HALO, rows // HALO
    vec = pl.BlockSpec((1, d), lambda t, s: (0, 0))
    return pl.pallas_call(
        functools.partial(_modulate_halo_kernel, tile=tile, sub=sub),
        grid=(n_t, n_s),
        in_specs=[
            pl.BlockSpec((sub, d), lambda t, s: (t * n_s + s, 0)),
            pl.BlockSpec((HALO, d), lambda t, s: (jnp.maximum(t * hb - 1, 0), 0)),
            pl.BlockSpec((HALO, d), lambda t, s: (jnp.minimum((t + 1) * hb, nblk - 1), 0)),
            vec, vec, vec,
        ],
        out_specs=pl.BlockSpec((None, tile + 2 * HALO, d), lambda t, s: (t, 0, 0)),
        out_shape=jax.ShapeDtypeStruct((n_t, tile + 2 * HALO, d), BF16),
        compiler_params=_params(4 * sub * d * 4 + 2 * (tile + 2 * HALO) * d * 2 + (8 << 20), 2),
        name="modulate_halo",
    )(x, x, x, g.reshape(1, d), scale.reshape(1, d), shift.reshape(1, d))


def _mm_kernel(*refs, n_seg, k_parts, halo, cast_w, lag, n_tiles, tm, epilogue, n_extra, n_out,
               k_dim):
    it = iter(refs)
    lhs_refs = [next(it) for _ in k_parts]
    w_refs = [next(it) for _ in range(n_seg)]
    extra_refs = [next(it) for _ in range(n_extra)]
    out_refs = [next(it) for _ in range(n_out)]
    wb_refs = [next(it) for _ in range(n_seg)] if cast_w else w_refs
    n_slots = 2 if lag else (1 if halo else 0)
    acc_slots = [[next(it) for _ in range(n_seg)] for _ in range(n_slots)]

    i = pl.program_id(1)
    if cast_w:
        ck = _pick(k_dim, (256, 128, 64, 32, 16))

        @pl.when(i == 0)
        def _():
            def cast_chunk(c, carry):
                rows = pl.ds(pl.multiple_of(c * ck, ck), ck)
                for w_ref, wb_ref in zip(w_refs, wb_refs):
                    wb_ref[rows, :] = w_ref[rows, :].astype(BF16)
                return carry
            lax.fori_loop(0, k_dim // ck, cast_chunk, 0)

    def dots(between=()):
        lhs_vals = [r[...] for r in lhs_refs]
        if between:
            kc = k_dim // len(between)
            accs = [None] * n_seg
            for c, epi_chunk in enumerate(between):
                for s, wb_ref in enumerate(wb_refs):
                    part = jnp.dot(lhs_vals[0][:, c * kc:(c + 1) * kc], wb_ref[c * kc:(c + 1) * kc, :],
                                   preferred_element_type=F32)
                    accs[s] = part if accs[s] is None else accs[s] + part
                epi_chunk()
            return accs
        accs = []
        for wb_ref in wb_refs:
            acc, k0 = None, 0
            for lhs, kp in zip(lhs_vals, k_parts):
                part = jnp.dot(lhs, wb_ref[k0:k0 + kp, :], preferred_element_type=F32)
                acc = part if acc is None else acc + part
                k0 += kp
            accs.append(acc)
        return accs

    def store(slot, between=()):
        for acc, ref in zip(dots(between), acc_slots[slot]):
            if halo:
                ref[HALO:, :] = acc[:tm + HALO, :]
                ref[:HALO, :] = acc[tm + HALO:, :]
            else:
                ref[...] = acc

    def finish(slot):
        accs = acc_slots[slot] if halo else [ref[...] for ref in acc_slots[slot]]
        epilogue(accs, extra_refs, out_refs)

    if not lag:
        if halo:
            store(0)
            finish(0)
        else:
            epilogue(dots(), extra_refs, out_refs)
        return


    @pl.when(i == 0)
    def _():
        store(0)

    for parity in range(2):
        @pl.when(jnp.logical_and(jnp.logical_and(i > 0, i < n_tiles), i % 2 == parity))
        def _():
            if halo:
                nr = max(8, tm // max(1, k_dim // MXU_DEPTH))
                store(parity, [functools.partial(epilogue, acc_slots[1 - parity], extra_refs, out_refs,
                                                 rows=(r0, nr)) for r0 in range(0, tm, nr)])
            else:
                finish(1 - parity)
                store(parity)

    @pl.when(i == n_tiles)
    def _():
        finish((n_tiles - 1) % 2)


def _mm(lhs, w, *, layer, seg_offs, n_cols, tn, tm, epilogue, extras, outs, halo=False, lag=False,
        name="mm"):
    lhs_parts = list(lhs) if isinstance(lhs, (list, tuple)) else [lhs]
    tiled = lhs_parts[0].ndim == 3
    if tiled:
        assert len(lhs_parts) == 1 and lhs_parts[0].shape[1] == tm + 2 * HALO
        rows = lhs_parts[0].shape[0] * tm
        k_parts = (lhs_parts[0].shape[2],)
    else:
        assert not halo
        rows = lhs_parts[0].shape[0]
        k_parts = tuple(p.shape[1] for p in lhs_parts)
    k_dim = sum(k_parts)
    assert rows % tm == 0 and n_cols % tn == 0
    n_tiles = rows // tm
    cast_w = w.dtype != BF16
    if lag:
        def dot_tile(i):
            return jnp.minimum(i, n_tiles - 1)

        def epi_tile(i):
            return jnp.maximum(i - 1, 0)
    else:
        def dot_tile(i):
            return i
        epi_tile = dot_tile
    if tiled:
        in_specs = [pl.BlockSpec((None, tm + 2 * HALO if halo else tm, k_dim),
                                 lambda j, i: (dot_tile(i), 0, 0))]
    else:
        in_specs = [pl.BlockSpec((tm, kp), lambda j, i: (dot_tile(i), 0)) for kp in k_parts]
    args = list(lhs_parts)
    for off in seg_offs:
        assert off % tn == 0
        ob = off // tn
        if w.ndim == 3:
            in_specs.append(pl.BlockSpec((None, k_dim, tn), lambda j, i, ob=ob: (layer, 0, ob + j)))
        else:
            in_specs.append(pl.BlockSpec((k_dim, tn), lambda j, i, ob=ob: (0, ob + j)))
        args.append(w)
    for arr, bs, im in extras:
        in_specs.append(pl.BlockSpec(bs, lambda j, i, im=im: im(j, epi_tile(i))))
        args.append(arr)
    out_specs = [pl.BlockSpec(bs, lambda j, i, im=im: im(j, epi_tile(i))) for _, bs, im in outs]
    out_shape = [sds for sds, _, _ in outs]
    n_seg = len(seg_offs)
    rows_l = tm + 2 * HALO if halo else tm
    scratch = []
    if cast_w:
        scratch += [pltpu.VMEM((k_dim, tn), BF16) for _ in range(n_seg)]
    n_slots = 2 if lag else (1 if halo else 0)
    scratch += [pltpu.VMEM((rows_l, tn), F32) for _ in range(n_slots * n_seg)]
    wbytes = w.dtype.itemsize
    est = (n_seg * k_dim * tn * (2 * wbytes + (2 if cast_w else 0))
           + 2 * rows_l * k_dim * 2 + (5 + 4 * n_seg) * rows_l * tn * 4 + (8 << 20))
    kern = functools.partial(
        _mm_kernel, n_seg=n_seg, k_parts=k_parts, halo=halo, cast_w=cast_w, lag=lag, n_tiles=n_tiles,
        tm=tm, epilogue=epilogue, n_extra=len(extras), n_out=len(outs), k_dim=k_dim)
    return pl.pallas_call(
        kern,
        grid=(n_cols // tn, n_tiles + (1 if lag else 0)),
        in_specs=in_specs,
        out_specs=out_specs,
        out_shape=out_shape,
        scratch_shapes=scratch,
        compiler_params=_params(est, 2),
        name=name,
    )(*args)


def _rms_lanes(x):
    return x * lax.rsqrt(jnp.mean(x * x, axis=-1, keepdims=True) + EPS)


def _epi_qk(accs, ex, outs, *, tn, scale):
    (acc,) = accs
    g_ref, cos_ref, sin_e_ref, sin_o_ref = ex
    (o_ref,) = outs
    g = g_ref[...]
    cos, sin_e, sin_o = cos_ref[...], sin_e_ref[...], sin_o_ref[...]
    for h in range(tn // LANES):
        hs = slice(h * LANES, (h + 1) * LANES)
        y = _rms_lanes(acc[:, hs]) * g
        r = y * cos + pltpu.roll(y, LANES - 1, 1) * sin_e + pltpu.roll(y, 1, 1) * sin_o
        o_ref[:, hs] = (r * scale if scale != 1.0 else r).astype(BF16)


def _epi_cast(accs, ex, outs):
    (acc,) = accs
    (o_ref,) = outs
    o_ref[...] = acc.astype(BF16)


def _epi_gmlp(accs, ex, outs, *, tm, tn):
    u, gv = accs
    gn_ref, ws_ref, bs_ref = ex
    (o_ref,) = outs
    gu = jax.nn.gelu(u)
    ggv = jax.nn.gelu(gv)
    nc = tm // LANES
    for g in range(tn // LANES):
        gs = slice(g * LANES, (g + 1) * LANES)
        vn = (_rms_lanes(ggv[:, gs]) * gn_ref[:, gs]).astype(BF16)
        ws = ws_ref[g].astype(BF16)
        bias = bs_ref[g]
        for c in range(nc):
            cs = slice(c * LANES, (c + 1) * LANES)
            res = jnp.dot(ws, vn[cs, :], preferred_element_type=F32)
            o_ref[cs, gs] = (gu[cs, gs] * (res + bias)).astype(BF16)


CONV_ROWS = 64


def _row_windows(ref, r0, nr):
    base = HALO + r0
    return ref[base - 1:base - 1 + nr, :], ref[base:base + nr, :], ref[base + 1:base + 1 + nr, :]


def _conv_chunks(tm, rows):
    nr = min(CONV_ROWS, tm)
    return [rows] if rows is not None else [(r0, nr) for r0 in range(0, tm, nr)]


def _epi_ffn_up(acc_refs, ex, outs, *, tm, rows=None):
    g_ref, u_ref = acc_refs
    cwg_ref, cwu_ref, cbg_ref, cbu_ref = ex
    (o_ref,) = outs
    for r0, nr in _conv_chunks(tm, rows):
        gp, gm, gn = _row_windows(g_ref, r0, nr)
        up, um, un = _row_windows(u_ref, r0, nr)
        ag = gp * cwg_ref[0:1, :] + gm * cwg_ref[1:2, :] + gn * cwg_ref[2:3, :] + cbg_ref[...]
        au = up * cwu_ref[0:1, :] + um * cwu_ref[1:2, :] + un * cwu_ref[2:3, :] + cbu_ref[...]
        o_ref[r0:r0 + nr, :] = (jax.nn.silu(ag) * au).astype(BF16)


def _epi_odd_conv(acc_refs, ex, outs, *, tm, rows=None):
    bg_ref, cg_ref, hx_ref = acc_refs
    (cw_ref,) = ex
    (o_ref,) = outs
    for r0, nr in _conv_chunks(tm, rows):
        cp, cm, cn = _row_windows(cg_ref, r0, nr)
        hp, hm, hn = _row_windows(hx_ref, r0, nr)
        conv = (cp * hp) * cw_ref[0:1, :] + (cm * hm) * cw_ref[1:2, :] + (cn * hn) * cw_ref[2:3, :]
        o_ref[r0:r0 + nr, :] = (bg_ref[HALO + r0:HALO + r0 + nr, :] * conv).astype(BF16)


def _epi_residual(accs, ex, outs):
    (acc,) = accs
    x_ref, gate_ref = ex
    (o_ref,) = outs
    o_ref[...] = x_ref[...] + gate_ref[...] * acc


def _epi_chan_dft(accs, ex, outs, *, tn):
    (acc,) = accs
    (cs_ref,) = ex
    (o_ref,) = outs
    cs = cs_ref[...]
    for g in range(tn // LANES):
        gs = slice(g * LANES, (g + 1) * LANES)
        ab = jnp.dot(acc[:, gs].astype(BF16), cs, preferred_element_type=F32)
        o_ref[0, :, gs] = ab[:, :LANES].astype(BF16)
        o_ref[1, :, gs] = ab[:, LANES:].astype(BF16)


def _epi_scale(accs, ex, outs, *, scale):
    (acc,) = accs
    (o_ref,) = outs
    o_ref[...] = (acc * scale).astype(BF16)


_NT = (((1,), (1,)), ((), ()))
_TN = (((0,), (0,)), ((), ()))


def _attn_kernel(*refs, tq, tk, n_lat, group):
    if n_lat:
        q_ref, kc_ref, vc_ref, k_ref, v_ref, o_ref = refs
    else:
        q_ref, kc_ref, vc_ref, o_ref = refs
    r = group * tq
    q = jnp.concatenate([q_ref[:, g * LANES:(g + 1) * LANES] for g in range(group)], axis=0)
    tiles = [(kc_ref, vc_ref, slice(None))]
    tiles += [(k_ref, v_ref, slice(t * tk, (t + 1) * tk)) for t in range(n_lat)]

    def scores(t):
        kr, _, rows = tiles[t]
        return lax.dot_general(kr[rows, :], q, _NT, preferred_element_type=F32)

    m = jnp.full((1, r), NEG_BIG, F32)
    l = jnp.zeros((1, r), F32)
    acc = jnp.zeros((LANES, r), F32)
    s_next = scores(0)
    for t in range(len(tiles)):
        s = s_next
        if t + 1 < len(tiles):
            s_next = scores(t + 1)
        _, vr, rows = tiles[t]
        m_new = jnp.maximum(m, jnp.max(s, axis=0, keepdims=True))
        alpha = jnp.exp2(m - m_new)
        p = jnp.exp2(s - m_new)
        l = alpha * l + jnp.sum(p, axis=0, keepdims=True)
        pv = lax.dot_general(vr[rows, :], p.astype(BF16), _TN, preferred_element_type=F32)
        acc = alpha * acc + pv
        m = m_new
    out_t = (acc * (1.0 / l)).T
    for g in range(group):
        o_ref[:, g * LANES:(g + 1) * LANES] = out_t[g * tq:(g + 1) * tq, :].astype(BF16)


def _attention(q, kc, vc, k_lat, v_lat, *, n_kv, group, tk):
    nq = q.shape[0]
    lc = kc.shape[0]
    tq = _pick(nq, (256, 128))
    gw = group * LANES
    ctx_spec = pl.BlockSpec((lc, LANES), lambda h, i: (0, h))
    in_specs = [pl.BlockSpec((tq, gw), lambda h, i: (i, h)), ctx_spec, ctx_spec]
    args = [q, kc, vc]
    n_lat = 0
    if k_lat is not None:
        n_lat = k_lat.shape[0] // tk
        lat_spec = pl.BlockSpec((k_lat.shape[0], LANES), lambda h, i: (0, h))
        in_specs += [lat_spec, lat_spec]
        args += [k_lat, v_lat]
    r = group * tq
    est = (4 * k_lat.shape[0] * LANES * 2 * 2 if n_lat else 0) + 6 * max(tk, lc) * r * 4 + (8 << 20)
    return pl.pallas_call(
        functools.partial(_attn_kernel, tq=tq, tk=tk, n_lat=n_lat, group=group),
        grid=(n_kv, nq // tq),
        in_specs=in_specs,
        out_specs=pl.BlockSpec((tq, gw), lambda h, i: (i, h)),
        out_shape=jax.ShapeDtypeStruct((nq, n_kv * gw), BF16),
        compiler_params=_params(est, 2),
        name="attention",
    )(*args)


def _rope_tables(n, identity):
    if identity:
        return jnp.ones((n, LANES), F32), jnp.zeros((n, LANES), F32), jnp.zeros((n, LANES), F32)
    rows = n // GRID_W
    row = jnp.repeat(jnp.arange(rows, dtype=F32), GRID_W)
    col = jnp.tile(jnp.arange(GRID_W, dtype=F32), rows)
    half = LANES // 2
    inv_freq = ROPE_THETA ** (-jnp.arange(0, half, 2, dtype=F32) / half)
    ang = jnp.concatenate([row[:, None] * inv_freq, col[:, None] * inv_freq], axis=-1)
    cos, sin = jnp.cos(ang), jnp.sin(ang)
    cos_i = jnp.repeat(cos, 2, axis=-1)
    sin_i = jnp.repeat(sin, 2, axis=-1)
    even = (jnp.arange(LANES) % 2 == 0)[None, :]
    return cos_i, jnp.where(even, -sin_i, 0.0), jnp.where(even, 0.0, sin_i)


def _dft_tables(n):
    idx = jnp.arange(n, dtype=jnp.int32)
    jk = (idx[:, None] * idx[None, :]) % n
    ang = jk.astype(F32) * (2.0 * math.pi / n)
    return jnp.cos(ang), jnp.sin(ang)


def _dft_lhs(n):
    n0 = LANES if n % LANES == 0 and n > LANES else 1
    n1 = n // n0
    kk = jnp.arange(2 * n, dtype=jnp.int32)
    k = kk % n
    phase = jnp.where(kk >= n, 0.5 * math.pi, 0.0).astype(F32)
    j1 = jnp.arange(n1, dtype=jnp.int32)[:, None]
    j0 = jnp.arange(n0, dtype=jnp.int32)[:, None]
    coarse = ((j1 * k[None, :]) % n1).astype(F32) * (2.0 * math.pi / n1)
    fine = ((j0 * k[None, :]) % n).astype(F32) * (2.0 * math.pi / n) + phase
    ca, sa = jnp.cos(coarse)[:, None, :], jnp.sin(coarse)[:, None, :]
    cb, sb = jnp.cos(fine)[None, :, :], jnp.sin(fine)[None, :, :]
    return (ca * cb - sa * sb).astype(BF16).reshape(n, 2 * n)


def _row_tile(rows):
    return _pick(rows, (1024, 512, 256, 128))


def _residual(lhs, w, layer, x, gate, name):
    parts = list(lhs) if isinstance(lhs, (list, tuple)) else [lhs]
    rows, k_dim = parts[0].shape[0], sum(p.shape[1] for p in parts)
    d = x.shape[1]
    if k_dim > 4096:
        tn, tm = _pick(d, (512, 256, 128)), _pick(rows, (512, 256, 128))
    else:
        tn, tm = _pick(d, (1024, 512, 256, 128)), _pick(rows, (512, 256, 128))
    (out,) = _mm(
        lhs, w, layer=layer, seg_offs=[0], n_cols=d, tn=tn, tm=tm, epilogue=_epi_residual,
        extras=[(x, (tm, tn), lambda j, i: (i, j)), (gate.reshape(1, d), (1, tn), lambda j, i: (0, j))],
        outs=[(jax.ShapeDtypeStruct((rows, d), F32), (tm, tn), lambda j, i: (i, j))],
        name=name)
    return out


def _ffn(x, mod, w_up, conv_w, conv_b, w_down, layer, gate):
    rows = x.shape[0]
    d_ff = w_down.shape[1]
    tn = _pick(d_ff, (512, 256, 128))
    tm = _pick(rows, (512, 256, 128))
    nb = d_ff // tn
    h2 = _modulate_halo(x, *mod, tile=tm)
    cw = conv_w[layer]
    cb = conv_b[layer].reshape(1, 2 * d_ff)
    (act,) = _mm(
        h2, w_up, layer=layer, seg_offs=[0, d_ff], n_cols=d_ff, tn=tn, tm=tm, halo=True,
        lag=rows > tm, epilogue=functools.partial(_epi_ffn_up, tm=tm),
        extras=[(cw, (3, tn), lambda j, i: (0, j)), (cw, (3, tn), lambda j, i: (0, nb + j)),
                (cb, (1, tn), lambda j, i: (0, j)), (cb, (1, tn), lambda j, i: (0, nb + j))],
        outs=[(jax.ShapeDtypeStruct((rows, d_ff), BF16), (tm, tn), lambda j, i: (i, j))],
        name="ffn_up")
    return _residual(act, w_down, layer, x, gate, "ffn_down")


def _even_in(h, w_in, e, q_g, k_g, gn_g, w_s, bs_b, tabs, dims, *, want_q, want_gmlp):
    a_q_w, a_kv_w, b_w = dims
    rows = h.shape[0]
    tm = _row_tile(rows)
    tn = _pick(a_kv_w, (512, 256, 128))
    cos, sin_e, sin_o = tabs
    tab_specs = [(t, (tm, LANES), lambda j, i: (i, 0)) for t in (cos, sin_e, sin_o)]
    q = gm = None

    def proj(off, n_cols, epilogue, extras, name, lag=False):
        (out,) = _mm(
            h, w_in, layer=e, seg_offs=[off], n_cols=n_cols, tn=tn, tm=tm, epilogue=epilogue,
            extras=extras, lag=lag and rows > tm,
            outs=[(jax.ShapeDtypeStruct((rows, n_cols), BF16), (tm, tn), lambda j, i: (i, j))],
            name=name)
        return out

    if want_q:
        qscale = float(LANES ** -0.5 * math.log2(math.e))
        q = proj(0, a_q_w, functools.partial(_epi_qk, tn=tn, scale=qscale),
                 [(q_g.reshape(1, LANES), (1, LANES), lambda j, i: (0, 0))] + tab_specs, "q_proj",
                 lag=True)
    k = proj(a_q_w, a_kv_w, functools.partial(_epi_qk, tn=tn, scale=1.0),
             [(k_g.reshape(1, LANES), (1, LANES), lambda j, i: (0, 0))] + tab_specs, "k_proj",
             lag=True)
    v = proj(a_q_w + a_kv_w, a_kv_w, _epi_cast, [], "v_proj")
    if want_gmlp:
        tg = _pick(b_w, (256, 128))
        ng = tg // LANES
        (gm,) = _mm(
            h, w_in, layer=e, seg_offs=[a_q_w + 2 * a_kv_w, a_q_w + 2 * a_kv_w + b_w], n_cols=b_w,
            tn=tg, tm=tm, epilogue=functools.partial(_epi_gmlp, tm=tm, tn=tg),
            extras=[(gn_g.reshape(1, b_w), (1, tg), lambda j, i: (0, j)),
                    (w_s, (ng, LANES, LANES), lambda j, i: (j, 0, 0)),
                    (bs_b, (ng, LANES, LANES), lambda j, i: (j, 0, 0))],
            outs=[(jax.ShapeDtypeStruct((rows, b_w), BF16), (tm, tg), lambda j, i: (i, j))],
            name="gmlp")
    return q, k, v, gm


def _epi_f32(accs, ex, outs):
    (acc,) = accs
    (o_ref,) = outs
    o_ref[...] = acc


FFT_GROUP = 8


def _fft_twiddle_kernel(y_ref, ct_ref, st_ref, o_ref, *, n2, ch):
    for g in range(FFT_GROUP):
        ct, st = ct_ref[g], st_ref[g]
        for c in range(ch // LANES):
            cols = slice(g * ch + c * LANES, g * ch + (c + 1) * LANES)
            sl = slice(c * LANES, (c + 1) * LANES)
            yr, yi = y_ref[:n2, cols], y_ref[n2:, cols]
            o_ref[0, g, :, sl] = (yr * ct - yi * st).astype(BF16)
            o_ref[1, g, :, sl] = (-(yi * ct) - yr * st).astype(BF16)


def _fourier_two_stage(f_ab, n_rows, d_w):
    n2 = LANES
    n1 = n_rows // n2
    assert n1 % FFT_GROUP == 0
    c2, s2 = _dft_tables(n2)
    m2 = jnp.concatenate([jnp.concatenate([c2, -s2], axis=1), jnp.concatenate([s2, c2], axis=1)],
                         axis=0).astype(BF16)
    tn = _pick(n1 * d_w, (2048, 1024, 512, 256, 128))
    (y,) = _mm(
        m2, f_ab.reshape(2 * n2, n1 * d_w), layer=0, seg_offs=[0], n_cols=n1 * d_w, tn=tn, tm=2 * n2,
        epilogue=_epi_f32, extras=[],
        outs=[(jax.ShapeDtypeStruct((2 * n2, n1 * d_w), F32), (2 * n2, tn), lambda j, i: (i, j))],
        name="fft_stage1")
    p1 = jnp.arange(n1, dtype=jnp.int32)[:, None]
    k2 = jnp.arange(n2, dtype=jnp.int32)[None, :]
    ang = ((p1 * k2) % n_rows).astype(F32) * (2.0 * math.pi / n_rows)
    ct = jnp.broadcast_to(jnp.cos(ang)[:, :, None], (n1, n2, LANES))
    st = jnp.broadcast_to(jnp.sin(ang)[:, :, None], (n1, n2, LANES))
    tw_spec = pl.BlockSpec((FFT_GROUP, n2, LANES), lambda a: (a, 0, 0))
    t = pl.pallas_call(
        functools.partial(_fft_twiddle_kernel, n2=n2, ch=d_w),
        grid=(n1 // FFT_GROUP,),
        in_specs=[pl.BlockSpec((2 * n2, FFT_GROUP * d_w), lambda a: (0, a)), tw_spec, tw_spec],
        out_specs=pl.BlockSpec((2, FFT_GROUP, n2, d_w), lambda a: (0, a, 0, 0)),
        out_shape=jax.ShapeDtypeStruct((2, n1, n2, d_w), BF16),
        compiler_params=_params(6 * 2 * n2 * FFT_GROUP * d_w * 4 + (8 << 20), 1),
        name="fft_twiddle",
    )(y, ct, st)
    c1, s1 = _dft_tables(n1)
    f1 = jnp.concatenate([c1, s1], axis=1).astype(BF16)
    scale = float(1.0 / math.sqrt(n_rows * LANES))
    tn3 = _pick(n2 * d_w, (4096, 2048, 1024, 512, 256, 128))
    (out,) = _mm(
        f1, t.reshape(2 * n1, n2 * d_w), layer=0, seg_offs=[0], n_cols=n2 * d_w, tn=tn3, tm=n1,
        epilogue=functools.partial(_epi_scale, scale=scale), extras=[],
        outs=[(jax.ShapeDtypeStruct((n1, n2 * d_w), BF16), (n1, tn3), lambda j, i: (i, j))],
        name="fft_stage3")
    return out.reshape(n_rows, d_w)


def _fourier(f_ab, n_rows, d_w, dft_lhs):
    if dft_lhs is None:
        return _fourier_two_stage(f_ab, n_rows, d_w)
    tm = _pick(n_rows, (256, 128))
    tn = _pick(d_w, (512, 256, 128))
    scale = float(1.0 / math.sqrt(n_rows * LANES))
    (out,) = _mm(
        dft_lhs, f_ab.reshape(2 * n_rows, d_w), layer=0, seg_offs=[0], n_cols=d_w, tn=tn, tm=tm,
        epilogue=functools.partial(_epi_scale, scale=scale), extras=[],
        outs=[(jax.ShapeDtypeStruct((n_rows, d_w), BF16), (tm, tn), lambda j, i: (i, j))],
        name="pos_dft")
    return out


def _odd_mixer(x, mod, w_in, o, conv_w, c_w, d_w, chan_cs, dft_lhs):
    rows = x.shape[0]
    tn = _pick(c_w, (256, 128))
    tm = _pick(rows, (512, 256, 128))
    h = _modulate_halo(x, *mod, tile=tm)
    (y_c,) = _mm(
        h, w_in, layer=o, seg_offs=[0, c_w, 2 * c_w], n_cols=c_w, tn=tn, tm=tm, halo=True,
        epilogue=functools.partial(_epi_odd_conv, tm=tm),
        extras=[(conv_w[o], (3, tn), lambda j, i: (0, j))],
        outs=[(jax.ShapeDtypeStruct((rows, c_w), BF16), (tm, tn), lambda j, i: (i, j))],
        name="odd_conv")
    tf = _pick(d_w, (512, 256, 128))
    tmf = tm
    (f_ab,) = _mm(
        h, w_in, layer=o, seg_offs=[3 * c_w], n_cols=d_w, tn=tf, tm=tmf,
        lag=rows > tmf, epilogue=functools.partial(_epi_chan_dft, tn=tf),
        extras=[(chan_cs, (LANES, 2 * LANES), lambda j, i: (0, 0))],
        outs=[(jax.ShapeDtypeStruct((2, rows, d_w), BF16), (2, tmf, tf), lambda j, i: (0, i, j))],
        name="chan_dft")
    four = _fourier(f_ab, rows, d_w, dft_lhs)
    return [y_c, four]


def kernel(x, c, ctx, c_ctx, w_ada, b_ada, norm1_g, norm2_g, w_ffn_up, ffn_conv_w, ffn_conv_b,
           w_ffn_down, w_in_a, w_out_a, q_norm_g, k_norm_g, gmlp_norm_g, gmlp_w_s, gmlp_b_s,
           w_in_c, w_out_c, conv_w_c):
    assert x.shape[0] == 1 and c.shape[0] == 1 and ctx.shape[0] == 1
    n, d = x.shape[1], x.shape[2]
    lc = ctx.shape[1]
    depth = w_ada.shape[0]
    assert q_norm_g.shape[-1] == LANES and gmlp_w_s.shape[2] == LANES
    b_w = gmlp_norm_g.shape[-1]
    a_q_w = w_out_a.shape[1] - b_w
    a_kv_w = (w_in_a.shape[-1] - a_q_w - 2 * b_w) // 2
    n_kv = a_kv_w // LANES
    group = a_q_w // a_kv_w
    c_w = conv_w_c.shape[-1]
    d_w = w_in_c.shape[-1] - 3 * c_w
    dims = (a_q_w, a_kv_w, b_w)

    x = x[0]
    xc = ctx[0]
    s_vecs = jnp.stack([jax.nn.silu(c[0]), jax.nn.silu(c_ctx)])
    ada = _ada(s_vecs, w_ada, b_ada)

    def mod_vecs(layer, stream):
        return [ada[layer, stream, p * d:(p + 1) * d] for p in range(6)]

    tk = _pick(n, (1024, 512, 256, 128))
    tabs_lat = _rope_tables(n, identity=False)
    tabs_ctx = _rope_tables(lc, identity=True)
    bs_b = jnp.broadcast_to(gmlp_b_s[..., None], gmlp_b_s.shape + (LANES,))

    any_odd = depth > 1
    if any_odd:
        cc, sc = _dft_tables(LANES)
        chan_cs = jnp.concatenate([cc, sc], axis=1).astype(BF16)
        dft_lat = None if n >= 16 * LANES else _dft_lhs(n)
        dft_ctx = None if lc >= 16 * LANES else _dft_lhs(lc)

    for i in range(depth):
        ctx_later = any(j % 2 == 0 for j in range(i + 1, depth))
        sh1, sc1, g1, sh2, sc2, g2 = mod_vecs(i, 0)
        csh1, csc1, cg1, csh2, csc2, cg2 = mod_vecs(i, 1)
        if i % 2 == 0:
            e = i // 2
            h = _modulate(x, norm1_g[i], sc1, sh1)
            hc = _modulate(xc, norm1_g[i], csc1, csh1)
            q, k, v, gm = _even_in(h, w_in_a, e, q_norm_g[e], k_norm_g[e], gmlp_norm_g[e],
                                   gmlp_w_s[e], bs_b[e], tabs_lat, dims,
                                   want_q=True, want_gmlp=True)
            qc, kc, vc, gmc = _even_in(hc, w_in_a, e, q_norm_g[e], k_norm_g[e], gmlp_norm_g[e],
                                       gmlp_w_s[e], bs_b[e], tabs_ctx, dims,
                                       want_q=ctx_later, want_gmlp=ctx_later)
            attn = _attention(q, kc, vc, k, v, n_kv=n_kv, group=group, tk=tk)
            mix = [attn, gm]
            x = _residual(mix, w_out_a, e, x, g1, "mix_out")
            if ctx_later:
                attn_c = _attention(qc, kc, vc, None, None, n_kv=n_kv, group=group, tk=tk)
                mix_c = [attn_c, gmc]
                xc = _residual(mix_c, w_out_a, e, xc, cg1, "mix_out")
        else:
            o = i // 2
            mix = _odd_mixer(x, (norm1_g[i], sc1, sh1), w_in_c, o, conv_w_c, c_w, d_w, chan_cs, dft_lat)
            x = _residual(mix, w_out_c, o, x, g1, "mix_out")
            if ctx_later:
                mix_c = _odd_mixer(xc, (norm1_g[i], csc1, csh1), w_in_c, o, conv_w_c, c_w, d_w,
                                   chan_cs, dft_ctx)
                xc = _residual(mix_c, w_out_c, o, xc, cg1, "mix_out")
        x = _ffn(x, (norm2_g[i], sc2, sh2), w_ffn_up, ffn_conv_w, ffn_conv_b, w_ffn_down, i, g2)
        if ctx_later:
            xc = _ffn(xc, (norm2_g[i], csc2, csh2), w_ffn_up, ffn_conv_w, ffn_conv_b, w_ffn_down, i, cg2)
    return x[None]
```

```python
import functools
import math

import jax
import jax.numpy as jnp
from jax import lax
from jax.experimental import pallas as pl
from jax.experimental.pallas import tpu as pltpu

BF16 = jnp.bfloat16
F32 = jnp.float32

GRID_W = 64
ROPE_THETA = 10000.0
EPS = 1e-6
LANES = 128
HALO = 8
MXU_DEPTH = 256
VMEM_CAP = 60 * 1024 * 1024
NEG_BIG = -1e30


def _pick(n, prefs):
    for p in prefs:
        if n % p == 0:
            return p
    return n


def _params(vmem_bytes, n_axes):
    return pltpu.CompilerParams(
        dimension_semantics=("arbitrary",) * n_axes,
        vmem_limit_bytes=int(min(VMEM_CAP, vmem_bytes)),
    )


def _ada_kernel(sb_ref, w_ref, b_ref, o_ref, *, tn, n_vec):
    for t in range(tn // LANES):
        sl = slice(t * LANES, (t + 1) * LANES)
        w = w_ref[:, sl]
        for r in range(n_vec):
            o_ref[r:r + 1, sl] = jnp.sum(w * sb_ref[r], axis=0, keepdims=True) + b_ref[:, sl]


def _ada(s_vecs, w_ada, b_ada):
    depth, d, n6 = w_ada.shape
    n_vec = s_vecs.shape[0]
    tn = _pick(n6, (1024, 512, 256, 128))
    sb = jnp.broadcast_to(s_vecs[:, :, None], (n_vec, d, LANES))
    return pl.pallas_call(
        functools.partial(_ada_kernel, tn=tn, n_vec=n_vec),
        grid=(depth, n6 // tn),
        in_specs=[
            pl.BlockSpec((n_vec, d, LANES), lambda l, j: (0, 0, 0)),
            pl.BlockSpec((None, d, tn), lambda l, j: (l, 0, j)),
            pl.BlockSpec((None, 1, tn), lambda l, j: (l, 0, j)),
        ],
        out_specs=pl.BlockSpec((None, n_vec, tn), lambda l, j: (l, 0, j)),
        out_shape=jax.ShapeDtypeStruct((depth, n_vec, n6), F32),
        compiler_params=_params(2 * d * tn * 4 + 4 * n_vec * d * LANES * 4 + (8 << 20), 2),
        name="ada",
    )(sb, w_ada, b_ada.reshape(depth, 1, n6))


def _modulate_kernel(x_ref, g_ref, sc_ref, sh_ref, o_ref):
    x = x_ref[...]
    ms = jnp.mean(x * x, axis=-1, keepdims=True)
    y = x * lax.rsqrt(ms + EPS)
    o_ref[...] = ((y * g_ref[...]) * (1.0 + sc_ref[...]) + sh_ref[...]).astype(BF16)


def _modulate(x, g, scale, shift):
    rows, d = x.shape
    tm = _pick(rows, (512, 256, 128, 64, 32, 16))
    vec = pl.BlockSpec((1, d), lambda i: (0, 0))
    return pl.pallas_call(
        _modulate_kernel,
        grid=(rows // tm,),
        in_specs=[pl.BlockSpec((tm, d), lambda i: (i, 0)), vec, vec, vec],
        out_specs=pl.BlockSpec((tm, d), lambda i: (i, 0)),
        out_shape=jax.ShapeDtypeStruct((rows, d), BF16),
        compiler_params=_params(6 * tm * d * 4 + (8 << 20), 1),
        name="modulate",
    )(x, g.reshape(1, d), scale.reshape(1, d), shift.reshape(1, d))


def _modulate_halo_kernel(x_ref, above_ref, below_ref, g_ref, sc_ref, sh_ref, o_ref, *, tile, sub):
    t, s = pl.program_id(0), pl.program_id(1)
    n_t, n_s = pl.num_programs(0), pl.num_programs(1)
    a = g_ref[...] * (1.0 + sc_ref[...])
    b = sh_ref[...]

    def mod(x):
        ms = jnp.mean(x * x, axis=-1, keepdims=True)
        return (x * lax.rsqrt(ms + EPS)) * a + b

    o_ref[pl.ds(pl.multiple_of(s * sub, sub), sub), :] = mod(x_ref[...]).astype(BF16)

    @pl.when(s == n_s - 1)
    def _():
        below = jnp.where(t < n_t - 1, mod(below_ref[...]), 0.0)
        above = jnp.where(t > 0, mod(above_ref[...]), 0.0)
        o_ref[tile:tile + 2 * HALO, :] = jnp.concatenate([below, above], axis=0).astype(BF16)


def _modulate_halo(x, g, scale, shift, tile):
    rows, d = x.shape
    assert rows % tile == 0 and tile % HALO == 0
    sub = _pick(tile, (512, 256, 128, 64, 32, 16))
    n_t, n_s = rows // tile, tile // sub
    hb, nblk = tile // HALO, rows // HALO
    vec = pl.BlockSpec((1, d), lambda t, s: (0, 0))
    return pl.pallas_call(
        functools.partial(_modulate_halo_kernel, tile=tile, sub=sub),
        grid=(n_t, n_s),
        in_specs=[
            pl.BlockSpec((sub, d), lambda t, s: (t * n_s + s, 0)),
            pl.BlockSpec((HALO, d), lambda t, s: (jnp.maximum(t * hb - 1, 0), 0)),
            pl.BlockSpec((HALO, d), lambda t, s: (jnp.minimum((t + 1) * hb, nblk - 1), 0)),
            vec, vec, vec,
        ],
        out_specs=pl.BlockSpec((None, tile + 2 * HALO, d), lambda t, s: (t, 0, 0)),
        out_shape=jax.ShapeDtypeStruct((n_t, tile + 2 * HALO, d), BF16),
        compiler_params=_params(4 * sub * d * 4 + 2 * (tile + 2 * HALO) * d * 2 + (8 << 20), 2),
        name="modulate_halo",
    )(x, x, x, g.reshape(1, d), scale.reshape(1, d), shift.reshape(1, d))


def _mm_kernel(*refs, n_seg, k_parts, halo, cast_w, lag, epi_rows, n_tiles, tm, epilogue, n_extra,
               n_out, k_dim):
    it = iter(refs)
    lhs_refs = [next(it) for _ in k_parts]
    w_refs = [next(it) for _ in range(n_seg)]
    extra_refs = [next(it) for _ in range(n_extra)]
    out_refs = [next(it) for _ in range(n_out)]
    wb_refs = [next(it) for _ in range(n_seg)] if cast_w else w_refs
    n_slots = 2 if lag else (1 if epi_rows else 0)
    acc_slots = [[next(it) for _ in range(n_seg)] for _ in range(n_slots)]

    i = pl.program_id(1)
    if cast_w:
        ck = _pick(k_dim, (256, 128, 64, 32, 16))

        @pl.when(i == 0)
        def _():
            def cast_chunk(c, carry):
                rows = pl.ds(pl.multiple_of(c * ck, ck), ck)
                for w_ref, wb_ref in zip(w_refs, wb_refs):
                    wb_ref[rows, :] = w_ref[rows, :].astype(BF16)
                return carry
            lax.fori_loop(0, k_dim // ck, cast_chunk, 0)

    def dots(between=()):
        lhs_vals = [r[...] for r in lhs_refs]
        if between:
            kc = k_dim // len(between)
            accs = [None] * n_seg
            for c, epi_chunk in enumerate(between):
                for s, wb_ref in enumerate(wb_refs):
                    part = jnp.dot(lhs_vals[0][:, c * kc:(c + 1) * kc], wb_ref[c * kc:(c + 1) * kc, :],
                                   preferred_element_type=F32)
                    accs[s] = part if accs[s] is None else accs[s] + part
                epi_chunk()
            return accs
        accs = []
        for wb_ref in wb_refs:
            acc, k0 = None, 0
            for lhs, kp in zip(lhs_vals, k_parts):
                part = jnp.dot(lhs, wb_ref[k0:k0 + kp, :], preferred_element_type=F32)
                acc = part if acc is None else acc + part
                k0 += kp
            accs.append(acc)
        return accs

    def store(slot, between=()):
        for acc, ref in zip(dots(between), acc_slots[slot]):
            if halo:
                ref[HALO:, :] = acc[:tm + HALO, :]
                ref[:HALO, :] = acc[tm + HALO:, :]
            else:
                ref[...] = acc

    def finish(slot):
        accs = acc_slots[slot] if epi_rows else [ref[...] for ref in acc_slots[slot]]
        epilogue(accs, extra_refs, out_refs)

    if not lag:
        if epi_rows:
            store(0)
            finish(0)
        else:
            epilogue(dots(), extra_refs, out_refs)
        return


    @pl.when(i == 0)
    def _():
        store(0)

    for parity in range(2):
        @pl.when(jnp.logical_and(jnp.logical_and(i > 0, i < n_tiles), i % 2 == parity))
        def _():
            if epi_rows:
                store(parity, [functools.partial(epilogue, acc_slots[1 - parity], extra_refs, out_refs,
                                                 rows=(r0, epi_rows)) for r0 in range(0, tm, epi_rows)])
            else:
                finish(1 - parity)
                store(parity)

    @pl.when(i == n_tiles)
    def _():
        finish((n_tiles - 1) % 2)


def _mm(lhs, w, *, layer, seg_offs, n_cols, tn, tm, epilogue, extras, outs, halo=False, lag=False,
        epi_rows=0, name="mm"):
    lhs_parts = list(lhs) if isinstance(lhs, (list, tuple)) else [lhs]
    tiled = lhs_parts[0].ndim == 3
    if tiled:
        assert len(lhs_parts) == 1 and lhs_parts[0].shape[1] == tm + 2 * HALO
        rows = lhs_parts[0].shape[0] * tm
        k_parts = (lhs_parts[0].shape[2],)
    else:
        assert not halo
        rows = lhs_parts[0].shape[0]
        k_parts = tuple(p.shape[1] for p in lhs_parts)
    k_dim = sum(k_parts)
    assert rows % tm == 0 and n_cols % tn == 0
    n_tiles = rows // tm
    cast_w = w.dtype != BF16
    if lag:
        def dot_tile(i):
            return jnp.minimum(i, n_tiles - 1)

        def epi_tile(i):
            return jnp.maximum(i - 1, 0)
    else:
        def dot_tile(i):
            return i
        epi_tile = dot_tile
    if tiled:
        in_specs = [pl.BlockSpec((None, tm + 2 * HALO if halo else tm, k_dim),
                                 lambda j, i: (dot_tile(i), 0, 0))]
    else:
        in_specs = [pl.BlockSpec((tm, kp), lambda j, i: (dot_tile(i), 0)) for kp in k_parts]
    args = list(lhs_parts)
    for off in seg_offs:
        assert off % tn == 0
        ob = off // tn
        if w.ndim == 3:
            in_specs.append(pl.BlockSpec((None, k_dim, tn), lambda j, i, ob=ob: (layer, 0, ob + j)))
        else:
            in_specs.append(pl.BlockSpec((k_dim, tn), lambda j, i, ob=ob: (0, ob + j)))
        args.append(w)
    for arr, bs, im in extras:
        in_specs.append(pl.BlockSpec(bs, lambda j, i, im=im: im(j, epi_tile(i))))
        args.append(arr)
    out_specs = [pl.BlockSpec(bs, lambda j, i, im=im: im(j, epi_tile(i))) for _, bs, im in outs]
    out_shape = [sds for sds, _, _ in outs]
    n_seg = len(seg_offs)
    rows_l = tm + 2 * HALO if halo else tm
    scratch = []
    if cast_w:
        scratch += [pltpu.VMEM((k_dim, tn), BF16) for _ in range(n_seg)]
    if halo and not epi_rows:
        epi_rows = max(8, tm // max(1, k_dim // MXU_DEPTH))
    n_slots = 2 if lag else (1 if epi_rows else 0)
    scratch += [pltpu.VMEM((rows_l, tn), F32) for _ in range(n_slots * n_seg)]
    wbytes = w.dtype.itemsize
    est = (n_seg * k_dim * tn * (2 * wbytes + (2 if cast_w else 0))
           + 2 * rows_l * k_dim * 2 + (5 + 4 * n_seg) * rows_l * tn * 4 + (8 << 20))
    kern = functools.partial(
        _mm_kernel, n_seg=n_seg, k_parts=k_parts, halo=halo, cast_w=cast_w, lag=lag, epi_rows=epi_rows,
        n_tiles=n_tiles,
        tm=tm, epilogue=epilogue, n_extra=len(extras), n_out=len(outs), k_dim=k_dim)
    return pl.pallas_call(
        kern,
        grid=(n_cols // tn, n_tiles + (1 if lag else 0)),
        in_specs=in_specs,
        out_specs=out_specs,
        out_shape=out_shape,
        scratch_shapes=scratch,
        compiler_params=_params(est, 2),
        name=name,
    )(*args)


def _rms_lanes(x):
    return x * lax.rsqrt(jnp.mean(x * x, axis=-1, keepdims=True) + EPS)


def _epi_qk(accs, ex, outs, *, tn, scale):
    (acc,) = accs
    g_ref, cos_ref, sin_e_ref, sin_o_ref = ex
    (o_ref,) = outs
    g = g_ref[...]
    cos, sin_e, sin_o = cos_ref[...], sin_e_ref[...], sin_o_ref[...]
    for h in range(tn // LANES):
        hs = slice(h * LANES, (h + 1) * LANES)
        y = _rms_lanes(acc[:, hs]) * g
        r = y * cos + pltpu.roll(y, LANES - 1, 1) * sin_e + pltpu.roll(y, 1, 1) * sin_o
        o_ref[:, hs] = (r * scale if scale != 1.0 else r).astype(BF16)


def _epi_cast(accs, ex, outs):
    (acc,) = accs
    (o_ref,) = outs
    o_ref[...] = acc.astype(BF16)


def _epi_gmlp(accs, ex, outs, *, tm, tn):
    u, gv = accs
    gn_ref, ws_ref, bs_ref = ex
    (o_ref,) = outs
    gu = jax.nn.gelu(u)
    ggv = jax.nn.gelu(gv)
    nc = tm // LANES
    for g in range(tn // LANES):
        gs = slice(g * LANES, (g + 1) * LANES)
        vn = (_rms_lanes(ggv[:, gs]) * gn_ref[:, gs]).astype(BF16)
        ws = ws_ref[g].astype(BF16)
        bias = bs_ref[g]
        for c in range(nc):
            cs = slice(c * LANES, (c + 1) * LANES)
            res = jnp.dot(ws, vn[cs, :], preferred_element_type=F32)
            o_ref[cs, gs] = (gu[cs, gs] * (res + bias)).astype(BF16)


CONV_ROWS = 64


def _row_windows(ref, r0, nr):
    base = HALO + r0
    return ref[base - 1:base - 1 + nr, :], ref[base:base + nr, :], ref[base + 1:base + 1 + nr, :]


def _conv_chunks(tm, rows):
    nr = min(CONV_ROWS, tm)
    return [rows] if rows is not None else [(r0, nr) for r0 in range(0, tm, nr)]


def _epi_ffn_up(acc_refs, ex, outs, *, tm, rows=None):
    g_ref, u_ref = acc_refs
    cwg_ref, cwu_ref, cbg_ref, cbu_ref = ex
    (o_ref,) = outs
    for r0, nr in _conv_chunks(tm, rows):
        gp, gm, gn = _row_windows(g_ref, r0, nr)
        up, um, un = _row_windows(u_ref, r0, nr)
        ag = gp * cwg_ref[0:1, :] + gm * cwg_ref[1:2, :] + gn * cwg_ref[2:3, :] + cbg_ref[...]
        au = up * cwu_ref[0:1, :] + um * cwu_ref[1:2, :] + un * cwu_ref[2:3, :] + cbu_ref[...]
        o_ref[r0:r0 + nr, :] = (jax.nn.silu(ag) * au).astype(BF16)


def _epi_odd_conv(acc_refs, ex, outs, *, tm, rows=None):
    bg_ref, cg_ref, hx_ref = acc_refs
    (cw_ref,) = ex
    (o_ref,) = outs
    for r0, nr in _conv_chunks(tm, rows):
        cp, cm, cn = _row_windows(cg_ref, r0, nr)
        hp, hm, hn = _row_windows(hx_ref, r0, nr)
        conv = (cp * hp) * cw_ref[0:1, :] + (cm * hm) * cw_ref[1:2, :] + (cn * hn) * cw_ref[2:3, :]
        o_ref[r0:r0 + nr, :] = (bg_ref[HALO + r0:HALO + r0 + nr, :] * conv).astype(BF16)


def _epi_residual(accs, ex, outs):
    (acc,) = accs
    x_ref, gate_ref = ex
    (o_ref,) = outs
    o_ref[...] = x_ref[...] + gate_ref[...] * acc


def _epi_chan_dft(accs, ex, outs, *, tn):
    (acc,) = accs
    (cs_ref,) = ex
    (o_ref,) = outs
    cs = cs_ref[...]
    for g in range(tn // LANES):
        gs = slice(g * LANES, (g + 1) * LANES)
        ab = jnp.dot(acc[:, gs].astype(BF16), cs, preferred_element_type=F32)
        o_ref[0, :, gs] = ab[:, :LANES].astype(BF16)
        o_ref[1, :, gs] = ab[:, LANES:].astype(BF16)


def _epi_scale(accs, ex, outs, *, scale):
    (acc,) = accs
    (o_ref,) = outs
    o_ref[...] = (acc * scale).astype(BF16)


_NT = (((1,), (1,)), ((), ()))
_TN = (((0,), (0,)), ((), ()))


def _attn_kernel(*refs, tq, tk, n_lat, group):
    if n_lat:
        q_ref, kc_ref, vc_ref, k_ref, v_ref, o_ref = refs
    else:
        q_ref, kc_ref, vc_ref, o_ref = refs
    r = group * tq
    q = jnp.concatenate([q_ref[:, g * LANES:(g + 1) * LANES] for g in range(group)], axis=0)
    tiles = [(kc_ref, vc_ref, slice(None))]
    tiles += [(k_ref, v_ref, slice(t * tk, (t + 1) * tk)) for t in range(n_lat)]

    def scores(t):
        kr, _, rows = tiles[t]
        return lax.dot_general(kr[rows, :], q, _NT, preferred_element_type=F32)

    m = jnp.full((1, r), NEG_BIG, F32)
    l = jnp.zeros((1, r), F32)
    acc = jnp.zeros((LANES, r), F32)
    s_next = scores(0)
    for t in range(len(tiles)):
        s = s_next
        if t + 1 < len(tiles):
            s_next = scores(t + 1)
        _, vr, rows = tiles[t]
        m_new = jnp.maximum(m, jnp.max(s, axis=0, keepdims=True))
        alpha = jnp.exp2(m - m_new)
        p = jnp.exp2(s - m_new)
        l = alpha * l + jnp.sum(p, axis=0, keepdims=True)
        pv = lax.dot_general(vr[rows, :], p.astype(BF16), _TN, preferred_element_type=F32)
        acc = alpha * acc + pv
        m = m_new
    out_t = (acc * (1.0 / l)).T
    for g in range(group):
        o_ref[:, g * LANES:(g + 1) * LANES] = out_t[g * tq:(g + 1) * tq, :].astype(BF16)


def _attention(q, kc, vc, k_lat, v_lat, *, n_kv, group, tk):
    nq = q.shape[0]
    lc = kc.shape[0]
    tq = _pick(nq, (256, 128))
    gw = group * LANES
    ctx_spec = pl.BlockSpec((lc, LANES), lambda h, i: (0, h))
    in_specs = [pl.BlockSpec((tq, gw), lambda h, i: (i, h)), ctx_spec, ctx_spec]
    args = [q, kc, vc]
    n_lat = 0
    if k_lat is not None:
        n_lat = k_lat.shape[0] // tk
        lat_spec = pl.BlockSpec((k_lat.shape[0], LANES), lambda h, i: (0, h))
        in_specs += [lat_spec, lat_spec]
        args += [k_lat, v_lat]
    r = group * tq
    est = (4 * k_lat.shape[0] * LANES * 2 * 2 if n_lat else 0) + 6 * max(tk, lc) * r * 4 + (8 << 20)
    return pl.pallas_call(
        functools.partial(_attn_kernel, tq=tq, tk=tk, n_lat=n_lat, group=group),
        grid=(n_kv, nq // tq),
        in_specs=in_specs,
        out_specs=pl.BlockSpec((tq, gw), lambda h, i: (i, h)),
        out_shape=jax.ShapeDtypeStruct((nq, n_kv * gw), BF16),
        compiler_params=_params(est, 2),
        name="attention",
    )(*args)


def _rope_tables(n, identity):
    if identity:
        return jnp.ones((n, LANES), F32), jnp.zeros((n, LANES), F32), jnp.zeros((n, LANES), F32)
    rows = n // GRID_W
    row = jnp.repeat(jnp.arange(rows, dtype=F32), GRID_W)
    col = jnp.tile(jnp.arange(GRID_W, dtype=F32), rows)
    half = LANES // 2
    inv_freq = ROPE_THETA ** (-jnp.arange(0, half, 2, dtype=F32) / half)
    ang = jnp.concatenate([row[:, None] * inv_freq, col[:, None] * inv_freq], axis=-1)
    cos, sin = jnp.cos(ang), jnp.sin(ang)
    cos_i = jnp.repeat(cos, 2, axis=-1)
    sin_i = jnp.repeat(sin, 2, axis=-1)
    even = (jnp.arange(LANES) % 2 == 0)[None, :]
    return cos_i, jnp.where(even, -sin_i, 0.0), jnp.where(even, 0.0, sin_i)


def _dft_tables(n):
    idx = jnp.arange(n, dtype=jnp.int32)
    jk = (idx[:, None] * idx[None, :]) % n
    ang = jk.astype(F32) * (2.0 * math.pi / n)
    return jnp.cos(ang), jnp.sin(ang)


def _dft_lhs(n):
    n0 = LANES if n % LANES == 0 and n > LANES else 1
    n1 = n // n0
    kk = jnp.arange(2 * n, dtype=jnp.int32)
    k = kk % n
    phase = jnp.where(kk >= n, 0.5 * math.pi, 0.0).astype(F32)
    j1 = jnp.arange(n1, dtype=jnp.int32)[:, None]
    j0 = jnp.arange(n0, dtype=jnp.int32)[:, None]
    coarse = ((j1 * k[None, :]) % n1).astype(F32) * (2.0 * math.pi / n1)
    fine = ((j0 * k[None, :]) % n).astype(F32) * (2.0 * math.pi / n) + phase
    ca, sa = jnp.cos(coarse)[:, None, :], jnp.sin(coarse)[:, None, :]
    cb, sb = jnp.cos(fine)[None, :, :], jnp.sin(fine)[None, :, :]
    return (ca * cb - sa * sb).astype(BF16).reshape(n, 2 * n)


def _row_tile(rows):
    return _pick(rows, (1024, 512, 256, 128))


def _residual(lhs, w, layer, x, gate, name):
    parts = list(lhs) if isinstance(lhs, (list, tuple)) else [lhs]
    rows, k_dim = parts[0].shape[0], sum(p.shape[1] for p in parts)
    d = x.shape[1]
    if k_dim > 4096:
        tn, tm = _pick(d, (512, 256, 128)), _pick(rows, (512, 256, 128))
    else:
        tn, tm = _pick(d, (1024, 512, 256, 128)), _pick(rows, (512, 256, 128))
    (out,) = _mm(
        lhs, w, layer=layer, seg_offs=[0], n_cols=d, tn=tn, tm=tm, epilogue=_epi_residual,
        extras=[(x, (tm, tn), lambda j, i: (i, j)), (gate.reshape(1, d), (1, tn), lambda j, i: (0, j))],
        outs=[(jax.ShapeDtypeStruct((rows, d), F32), (tm, tn), lambda j, i: (i, j))],
        name=name)
    return out


def _ffn(x, mod, w_up, conv_w, conv_b, w_down, layer, gate):
    rows = x.shape[0]
    d_ff = w_down.shape[1]
    tn = _pick(d_ff, (512, 256, 128))
    tm = _pick(rows, (512, 256, 128))
    nb = d_ff // tn
    h2 = _modulate_halo(x, *mod, tile=tm)
    cw = conv_w[layer]
    cb = conv_b[layer].reshape(1, 2 * d_ff)
    (act,) = _mm(
        h2, w_up, layer=layer, seg_offs=[0, d_ff], n_cols=d_ff, tn=tn, tm=tm, halo=True,
        lag=rows > tm, epilogue=functools.partial(_epi_ffn_up, tm=tm),
        extras=[(cw, (3, tn), lambda j, i: (0, j)), (cw, (3, tn), lambda j, i: (0, nb + j)),
                (cb, (1, tn), lambda j, i: (0, j)), (cb, (1, tn), lambda j, i: (0, nb + j))],
        outs=[(jax.ShapeDtypeStruct((rows, d_ff), BF16), (tm, tn), lambda j, i: (i, j))],
        name="ffn_up")
    return _residual(act, w_down, layer, x, gate, "ffn_down")


def _even_in(h, w_in, e, q_g, k_g, gn_g, w_s, bs_b, tabs, dims, *, want_q, want_gmlp):
    a_q_w, a_kv_w, b_w = dims
    rows = h.shape[0]
    tm = _row_tile(rows)
    tn = _pick(a_kv_w, (512, 256, 128))
    cos, sin_e, sin_o = tabs
    tab_specs = [(t, (tm, LANES), lambda j, i: (i, 0)) for t in (cos, sin_e, sin_o)]
    q = gm = None

    def proj(off, n_cols, epilogue, extras, name, lag=False):
        (out,) = _mm(
            h, w_in, layer=e, seg_offs=[off], n_cols=n_cols, tn=tn, tm=tm, epilogue=epilogue,
            extras=extras, lag=lag and rows > tm,
            outs=[(jax.ShapeDtypeStruct((rows, n_cols), BF16), (tm, tn), lambda j, i: (i, j))],
            name=name)
        return out

    if want_q:
        qscale = float(LANES ** -0.5 * math.log2(math.e))
        q = proj(0, a_q_w, functools.partial(_epi_qk, tn=tn, scale=qscale),
                 [(q_g.reshape(1, LANES), (1, LANES), lambda j, i: (0, 0))] + tab_specs, "q_proj",
                 lag=True)
    k = proj(a_q_w, a_kv_w, functools.partial(_epi_qk, tn=tn, scale=1.0),
             [(k_g.reshape(1, LANES), (1, LANES), lambda j, i: (0, 0))] + tab_specs, "k_proj",
             lag=True)
    v = proj(a_q_w + a_kv_w, a_kv_w, _epi_cast, [], "v_proj")
    if want_gmlp:
        tg = _pick(b_w, (256, 128))
        ng = tg // LANES
        (gm,) = _mm(
            h, w_in, layer=e, seg_offs=[a_q_w + 2 * a_kv_w, a_q_w + 2 * a_kv_w + b_w], n_cols=b_w,
            tn=tg, tm=tm, epilogue=functools.partial(_epi_gmlp, tm=tm, tn=tg),
            extras=[(gn_g.reshape(1, b_w), (1, tg), lambda j, i: (0, j)),
                    (w_s, (ng, LANES, LANES), lambda j, i: (j, 0, 0)),
                    (bs_b, (ng, LANES, LANES), lambda j, i: (j, 0, 0))],
            outs=[(jax.ShapeDtypeStruct((rows, b_w), BF16), (tm, tg), lambda j, i: (i, j))],
            name="gmlp")
    return q, k, v, gm


def _epi_f32(accs, ex, outs):
    (acc,) = accs
    (o_ref,) = outs
    o_ref[...] = acc


FFT_GROUP = 8


def _fft_twiddle_kernel(y_ref, ct_ref, st_ref, o_ref, *, n2, ch):
    for g in range(FFT_GROUP):
        ct, st = ct_ref[g], st_ref[g]
        for c in range(ch // LANES):
            cols = slice(g * ch + c * LANES, g * ch + (c + 1) * LANES)
            sl = slice(c * LANES, (c + 1) * LANES)
            yr, yi = y_ref[:n2, cols], y_ref[n2:, cols]
            o_ref[0, g, :, sl] = (yr * ct - yi * st).astype(BF16)
            o_ref[1, g, :, sl] = (-(yi * ct) - yr * st).astype(BF16)


def _fourier_two_stage(f_ab, n_rows, d_w):
    n2 = LANES
    n1 = n_rows // n2
    assert n1 % FFT_GROUP == 0
    c2, s2 = _dft_tables(n2)
    m2 = jnp.concatenate([jnp.concatenate([c2, -s2], axis=1), jnp.concatenate([s2, c2], axis=1)],
                         axis=0).astype(BF16)
    tn = _pick(n1 * d_w, (4096, 2048, 1024, 512, 256, 128))
    (y,) = _mm(
        m2, f_ab.reshape(2 * n2, n1 * d_w), layer=0, seg_offs=[0], n_cols=n1 * d_w, tn=tn, tm=2 * n2,
        epilogue=_epi_f32, extras=[],
        outs=[(jax.ShapeDtypeStruct((2 * n2, n1 * d_w), F32), (2 * n2, tn), lambda j, i: (i, j))],
        name="fft_stage1")
    p1 = jnp.arange(n1, dtype=jnp.int32)[:, None]
    k2 = jnp.arange(n2, dtype=jnp.int32)[None, :]
    ang = ((p1 * k2) % n_rows).astype(F32) * (2.0 * math.pi / n_rows)
    ct = jnp.broadcast_to(jnp.cos(ang)[:, :, None], (n1, n2, LANES))
    st = jnp.broadcast_to(jnp.sin(ang)[:, :, None], (n1, n2, LANES))
    tw_spec = pl.BlockSpec((FFT_GROUP, n2, LANES), lambda a: (a, 0, 0))
    t = pl.pallas_call(
        functools.partial(_fft_twiddle_kernel, n2=n2, ch=d_w),
        grid=(n1 // FFT_GROUP,),
        in_specs=[pl.BlockSpec((2 * n2, FFT_GROUP * d_w), lambda a: (0, a)), tw_spec, tw_spec],
        out_specs=pl.BlockSpec((2, FFT_GROUP, n2, d_w), lambda a: (0, a, 0, 0)),
        out_shape=jax.ShapeDtypeStruct((2, n1, n2, d_w), BF16),
        compiler_params=_params(6 * 2 * n2 * FFT_GROUP * d_w * 4 + (8 << 20), 1),
        name="fft_twiddle",
    )(y, ct, st)
    c1, s1 = _dft_tables(n1)
    f1 = jnp.concatenate([c1, s1], axis=1).astype(BF16)
    scale = float(1.0 / math.sqrt(n_rows * LANES))
    tn3 = _pick(n2 * d_w, (8192, 4096, 2048, 1024, 512, 256, 128))
    (out,) = _mm(
        f1, t.reshape(2 * n1, n2 * d_w), layer=0, seg_offs=[0], n_cols=n2 * d_w, tn=tn3, tm=n1,
        epilogue=functools.partial(_epi_scale, scale=scale), extras=[],
        outs=[(jax.ShapeDtypeStruct((n1, n2 * d_w), BF16), (n1, tn3), lambda j, i: (i, j))],
        name="fft_stage3")
    return out.reshape(n_rows, d_w)


def _fourier(f_ab, n_rows, d_w, dft_lhs):
    if dft_lhs is None:
        return _fourier_two_stage(f_ab, n_rows, d_w)
    tm = _pick(n_rows, (256, 128))
    tn = _pick(d_w, (512, 256, 128))
    scale = float(1.0 / math.sqrt(n_rows * LANES))
    (out,) = _mm(
        dft_lhs, f_ab.reshape(2 * n_rows, d_w), layer=0, seg_offs=[0], n_cols=d_w, tn=tn, tm=tm,
        epilogue=functools.partial(_epi_scale, scale=scale), extras=[],
        outs=[(jax.ShapeDtypeStruct((n_rows, d_w), BF16), (tm, tn), lambda j, i: (i, j))],
        name="pos_dft")
    return out


def _odd_mixer(x, mod, w_in, o, conv_w, c_w, d_w, chan_cs, dft_lhs):
    rows = x.shape[0]
    tn = _pick(c_w, (256, 128))
    tm = _pick(rows, (512, 256, 128))
    h = _modulate_halo(x, *mod, tile=tm)
    (y_c,) = _mm(
        h, w_in, layer=o, seg_offs=[0, c_w, 2 * c_w], n_cols=c_w, tn=tn, tm=tm, halo=True,
        epilogue=functools.partial(_epi_odd_conv, tm=tm),
        extras=[(conv_w[o], (3, tn), lambda j, i: (0, j))],
        outs=[(jax.ShapeDtypeStruct((rows, c_w), BF16), (tm, tn), lambda j, i: (i, j))],
        name="odd_conv")
    tf = _pick(d_w, (512, 256, 128))
    tmf = tm
    (f_ab,) = _mm(
        h, w_in, layer=o, seg_offs=[3 * c_w], n_cols=d_w, tn=tf, tm=tmf,
        lag=rows > tmf, epilogue=functools.partial(_epi_chan_dft, tn=tf),
        extras=[(chan_cs, (LANES, 2 * LANES), lambda j, i: (0, 0))],
        outs=[(jax.ShapeDtypeStruct((2, rows, d_w), BF16), (2, tmf, tf), lambda j, i: (0, i, j))],
        name="chan_dft")
    four = _fourier(f_ab, rows, d_w, dft_lhs)
    return [y_c, four]


def kernel(x, c, ctx, c_ctx, w_ada, b_ada, norm1_g, norm2_g, w_ffn_up, ffn_conv_w, ffn_conv_b,
           w_ffn_down, w_in_a, w_out_a, q_norm_g, k_norm_g, gmlp_norm_g, gmlp_w_s, gmlp_b_s,
           w_in_c, w_out_c, conv_w_c):
    assert x.shape[0] == 1 and c.shape[0] == 1 and ctx.shape[0] == 1
    n, d = x.shape[1], x.shape[2]
    lc = ctx.shape[1]
    depth = w_ada.shape[0]
    assert q_norm_g.shape[-1] == LANES and gmlp_w_s.shape[2] == LANES
    b_w = gmlp_norm_g.shape[-1]
    a_q_w = w_out_a.shape[1] - b_w
    a_kv_w = (w_in_a.shape[-1] - a_q_w - 2 * b_w) // 2
    n_kv = a_kv_w // LANES
    group = a_q_w // a_kv_w
    c_w = conv_w_c.shape[-1]
    d_w = w_in_c.shape[-1] - 3 * c_w
    dims = (a_q_w, a_kv_w, b_w)

    x = x[0]
    xc = ctx[0]
    s_vecs = jnp.stack([jax.nn.silu(c[0]), jax.nn.silu(c_ctx)])
    ada = _ada(s_vecs, w_ada, b_ada)

    def mod_vecs(layer, stream):
        return [ada[layer, stream, p * d:(p + 1) * d] for p in range(6)]

    tk = _pick(n, (1024, 512, 256, 128))
    tabs_lat = _rope_tables(n, identity=False)
    tabs_ctx = _rope_tables(lc, identity=True)
    bs_b = jnp.broadcast_to(gmlp_b_s[..., None], gmlp_b_s.shape + (LANES,))

    any_odd = depth > 1
    if any_odd:
        cc, sc = _dft_tables(LANES)
        chan_cs = jnp.concatenate([cc, sc], axis=1).astype(BF16)
        dft_lat = None if n >= 16 * LANES else _dft_lhs(n)
        dft_ctx = None if lc >= 16 * LANES else _dft_lhs(lc)

    for i in range(depth):
        ctx_later = any(j % 2 == 0 for j in range(i + 1, depth))
        sh1, sc1, g1, sh2, sc2, g2 = mod_vecs(i, 0)
        csh1, csc1, cg1, csh2, csc2, cg2 = mod_vecs(i, 1)
        if i % 2 == 0:
            e = i // 2
            h = _modulate(x, norm1_g[i], sc1, sh1)
            hc = _modulate(xc, norm1_g[i], csc1, csh1)
            q, k, v, gm = _even_in(h, w_in_a, e, q_norm_g[e], k_norm_g[e], gmlp_norm_g[e],
                                   gmlp_w_s[e], bs_b[e], tabs_lat, dims,
                                   want_q=True, want_gmlp=True)
            qc, kc, vc, gmc = _even_in(hc, w_in_a, e, q_norm_g[e], k_norm_g[e], gmlp_norm_g[e],
                                       gmlp_w_s[e], bs_b[e], tabs_ctx, dims,
                                       want_q=ctx_later, want_gmlp=ctx_later)
            attn = _attention(q, kc, vc, k, v, n_kv=n_kv, group=group, tk=tk)
            mix = [attn, gm]
            x = _residual(mix, w_out_a, e, x, g1, "mix_out")
            if ctx_later:
                attn_c = _attention(qc, kc, vc, None, None, n_kv=n_kv, group=group, tk=tk)
                mix_c = [attn_c, gmc]
                xc = _residual(mix_c, w_out_a, e, xc, cg1, "mix_out")
        else:
            o = i // 2
            mix = _odd_mixer(x, (norm1_g[i], sc1, sh1), w_in_c, o, conv_w_c, c_w, d_w, chan_cs, dft_lat)
            x = _residual(mix, w_out_c, o, x, g1, "mix_out")
            if ctx_later:
                mix_c = _odd_mixer(xc, (norm1_g[i], csc1, csh1), w_in_c, o, conv_w_c, c_w, d_w,
                                   chan_cs, dft_ctx)
                xc = _residual(mix_c, w_out_c, o, xc, cg1, "mix_out")
        x = _ffn(x, (norm2_g[i], sc2, sh2), w_ffn_up, ffn_conv_w, ffn_conv_b, w_ffn_down, i, g2)
        if ctx_later:
            xc = _ffn(xc, (norm2_g[i], csc2, csh2), w_ffn_up, ffn_conv_w, ffn_conv_b, w_ffn_down, i, cg2)
    return x[None]
```
